```python
import functools
import jax, jax.numpy as jnp
from jax import lax
import numpy as np

D_MODEL = 1024
BATCH = 2
SEQ = 8192
DEPTH = 4
DEC_BATCH = 32
DEC_SEQ = 4
PAST_LEN = 8192
PAGE_SIZE = 128

N_EVEN = (DEPTH + 1) // 2
N_ODD = DEPTH // 2
W_A = D_MODEL
H_A = 8
BW_A = W_A // H_A
LRU_CONV = 4
LRU_C = 8.0
H_B = 16
HD_B = D_MODEL // H_B
W_B = H_B * HD_B
MOBA_BLOCK = 256
MOBA_TOPK = 3
Q_CHUNK = 16
W_C = D_MODEL
POOL_WINDOWS = (2, 4, 8, 16)
N_POOL_GROUPS = len(POOL_WINDOWS)
GW_C = W_C // N_POOL_GROUPS
POOL_BUF = max(POOL_WINDOWS) - 1
W_D = D_MODEL
D_CONV = 31
IN_EVEN = 2 * W_A + 4 * W_B
IN_ODD = 2 * W_C + 3 * W_D
MIX_EVEN = W_A + W_B
MIX_ODD = W_C + W_D
EPS = 1e-6
NEG = -1e30

kernel_name = 'hybrid_rglru_moba_pool_conformer_step'


def rmsnorm(x, g):
    xf = x.astype(jnp.float32)
    y = xf * lax.rsqrt(jnp.mean(xf * xf, axis=-1, keepdims=True) + EPS)
    return (y * g.astype(jnp.float32)).astype(x.dtype)


def layernorm(x, g, b):
    xf = x.astype(jnp.float32)
    xc = xf - jnp.mean(xf, axis=-1, keepdims=True)
    var = jnp.mean(xc * xc, axis=-1, keepdims=True)
    return (xc * lax.rsqrt(var + EPS) * g.astype(jnp.float32) + b.astype(jnp.float32)).astype(x.dtype)


def alibi_slopes(n):
    return 2.0 ** (-8.0 * (jnp.arange(n, dtype=jnp.float32) + 1.0) / n)


def causal_dwconv(u, prefix, w, b):
    k = w.shape[0]
    c = u.shape[-1]
    up = jnp.concatenate([prefix.astype(u.dtype), u], axis=1)
    y = lax.conv_general_dilated(up, w[:, None, :].astype(u.dtype), window_strides=(1,), padding='VALID',
                                 dimension_numbers=('NWC', 'WIO', 'NWC'), feature_group_count=c)
    return y + b.astype(u.dtype), up[:, -(k - 1):]


def rglru(u, h0, wr, br, wi, bi, lam):
    n, t, w = u.shape
    ub = u.reshape(n, t, H_A, BW_A)
    r = jax.nn.sigmoid(jnp.einsum('nthi,hij->nthj', ub, wr).reshape(n, t, w) + br)
    i = jax.nn.sigmoid(jnp.einsum('nthi,hij->nthj', ub, wi).reshape(n, t, w) + bi)
    log_a = -LRU_C * jax.nn.softplus(-lam.astype(jnp.float32)) * r.astype(jnp.float32)
    a = jnp.exp(log_a)
    bx = jnp.sqrt(-jnp.expm1(2.0 * log_a)) * (i * u).astype(jnp.float32)
    bx = bx.at[:, 0].add(a[:, 0] * h0.astype(jnp.float32))

    def combine(lhs, rhs):
        a1, b1 = lhs
        a2, b2 = rhs
        return a1 * a2, a2 * b1 + b2

    _, h = lax.associative_scan(combine, (a, bx), axis=1)
    return h.astype(u.dtype), h[:, -1].astype(u.dtype)


def block_means(kf):
    n, t, h, d = kf.shape
    return kf.reshape(n, t // MOBA_BLOCK, MOBA_BLOCK, h, d).astype(jnp.float32).mean(axis=2).astype(kf.dtype)


def moba_attend(q, kf, vf, k_mean, pos, slopes):
    n, tq, h, d = q.shape
    nb = k_mean.shape[1]
    kb = kf.reshape(n, nb, MOBA_BLOCK, h, d)
    vb = vf.reshape(n, nb, MOBA_BLOCK, h, d)
    own = pos // MOBA_BLOCK
    gate = jnp.einsum('nthd,nbhd->nthb', q, k_mean).astype(jnp.float32)
    is_past = jnp.arange(nb, dtype=jnp.int32)[None, :] < own[:, None]
    gate = jnp.where(is_past[None, :, None, :], gate, NEG)
    _, sel = lax.top_k(gate, min(MOBA_TOPK, nb))
    own_b = jnp.broadcast_to(own[None, :, None, None], (n, tq, h, 1)).astype(sel.dtype)
    blk = jnp.concatenate([sel, own_b], axis=-1)
    ok = jnp.concatenate([sel < own[None, :, None, None], jnp.ones(own_b.shape, dtype=bool)], axis=-1)
    n_i = jnp.arange(n)[:, None, None, None]
    h_i = jnp.arange(h)[None, None, :, None]
    kg = kb[n_i, blk, :, h_i, :]
    vg = vb[n_i, blk, :, h_i, :]
    key_pos = blk[..., None] * MOBA_BLOCK + jnp.arange(MOBA_BLOCK, dtype=blk.dtype)
    dist = pos[None, :, None, None, None] - key_pos
    valid = ok[..., None] & (dist >= 0)
    s = (jnp.einsum('nthd,nthjkd->nthjk', q, kg).astype(jnp.float32) * (d ** -0.5)
         - slopes[None, None, :, None, None] * dist.astype(jnp.float32))
    s = jnp.where(valid, s, NEG).reshape(n, tq, h, -1)
    p = jax.nn.softmax(s, axis=-1).astype(vg.dtype)
    return jnp.einsum('nthm,nthmd->nthd', p, vg.reshape(n, tq, h, -1, d))


def moba_prompt(q, k, v, slopes):
    n, s, h, d = q.shape
    pad = (-s) % MOBA_BLOCK
    kf = jnp.pad(k, ((0, 0), (0, pad), (0, 0), (0, 0)))
    vf = jnp.pad(v, ((0, 0), (0, pad), (0, 0), (0, 0)))
    km = block_means(kf)

    def one_block(c):
        start = c * Q_CHUNK
        qc = lax.dynamic_slice_in_dim(q, start, Q_CHUNK, axis=1)
        pos = start + jnp.arange(Q_CHUNK, dtype=jnp.int32)
        return moba_attend(qc, kf, vf, km, pos, slopes)

    o = lax.map(one_block, jnp.arange(s // Q_CHUNK, dtype=jnp.int32))
    return o.transpose(1, 0, 2, 3, 4).reshape(n, s, h, d)


def moba_sample(q, k, v, k_past, v_past, slopes):
    n, t, h, d = q.shape
    p_len = k_past.shape[1]
    pad = (-(p_len + t)) % MOBA_BLOCK
    zeros = jnp.zeros((n, pad, h, d), k.dtype)
    kf = jnp.concatenate([k_past.astype(k.dtype), k, zeros], axis=1)
    vf = jnp.concatenate([v_past.astype(v.dtype), v, zeros.astype(v.dtype)], axis=1)
    pos = p_len + jnp.arange(t, dtype=jnp.int32)
    return moba_attend(q, kf, vf, block_means(kf), pos, slopes)


def pool_mix(u, prefix, pos0):
    n, t, w = u.shape
    up = jnp.concatenate([prefix.astype(u.dtype), u], axis=1)
    cs = jnp.pad(jnp.cumsum(up.astype(jnp.float32), axis=1), ((0, 0), (1, 0), (0, 0)))
    pos = pos0 + jnp.arange(t, dtype=jnp.int32)
    outs = []
    for g, win in enumerate(POOL_WINDOWS):
        lo, hi = g * GW_C, (g + 1) * GW_C
        ssum = cs[:, POOL_BUF + 1:, lo:hi] - cs[:, POOL_BUF + 1 - win:POOL_BUF + 1 - win + t, lo:hi]
        cnt = jnp.minimum(win, pos + 1).astype(jnp.float32)
        outs.append(ssum / cnt[None, :, None])
    pooled = jnp.concatenate(outs, axis=-1) - u.astype(jnp.float32)
    return pooled.astype(u.dtype), up[:, -POOL_BUF:]


def even_mixer(h, attend, conv_prefix, h0, w_in, conv_w, conv_b, wr, br, wi, bi, lam, w_out):
    n, t, _ = h.shape
    z = h @ w_in
    xa, ga, q, k, v, gb = jnp.split(z, [W_A, 2 * W_A, 2 * W_A + W_B, 2 * W_A + 2 * W_B, 2 * W_A + 3 * W_B], axis=-1)
    xc, conv_state = causal_dwconv(xa, conv_prefix, conv_w, conv_b)
    ha, h_last = rglru(xc, h0, wr, br, wi, bi, lam)
    q = q.reshape(n, t, H_B, HD_B)
    k = k.reshape(n, t, H_B, HD_B)
    v = v.reshape(n, t, H_B, HD_B)
    o = attend(q, k, v).reshape(n, t, W_B)
    mixed = jnp.concatenate([ha * jax.nn.silu(ga), o * jax.nn.silu(gb)], axis=-1)
    return mixed @ w_out, k, v, h_last, conv_state


def odd_mixer(h, pos0, pool_prefix, dconv_prefix, w_in, pool_w, pool_b, pool_scale, dconv_w, dconv_b, dln_g, dln_b, w_out):
    n, t, _ = h.shape
    z = h @ w_in
    xc, gc, da, db, gd = jnp.split(z, [W_C, 2 * W_C, 2 * W_C + W_D, 2 * W_C + 2 * W_D], axis=-1)
    pooled, pool_state = pool_mix(xc, pool_prefix, pos0)
    yc = jnp.einsum('ntgi,gij->ntgj', pooled.reshape(n, t, N_POOL_GROUPS, GW_C), pool_w).reshape(n, t, W_C)
    yc = (yc + pool_b) * pool_scale
    u = da * jax.nn.sigmoid(db)
    dc, dconv_state = causal_dwconv(u, dconv_prefix, dconv_w, dconv_b)
    yd = jax.nn.silu(layernorm(dc, dln_g, dln_b))
    mixed = jnp.concatenate([yc * jax.nn.silu(gc), yd * jax.nn.silu(gd)], axis=-1)
    return mixed @ w_out, pool_state, dconv_state


def setup_inputs(seed: int = 0) -> dict:
    key = jax.random.key(seed)
    ks = list(jax.random.split(key, 32))

    def nrm(i, shape, scale):
        return jax.random.normal(ks[i], shape, jnp.float32) * scale

    n_pages = PAST_LEN // PAGE_SIZE
    n_used = DEC_BATCH * n_pages
    n_phys = n_used + max(1, n_used // 4)
    perm = jax.random.permutation(ks[0], n_phys)
    page_table = perm[:n_used].reshape(DEC_BATCH, n_pages).astype(jnp.int32)
    u = jax.random.uniform(ks[1], (N_EVEN, W_A), jnp.float32, 0.9, 0.999)
    base = u ** (1.0 / LRU_C)
    lru_lambda = jnp.log(base) - jnp.log1p(-base)
    return {
        'x_prompt': nrm(2, (BATCH, SEQ, D_MODEL), 1.0),
        'x_sample': nrm(3, (DEC_BATCH, DEC_SEQ, D_MODEL), 1.0),
        'cache_k': nrm(4, (n_phys, N_EVEN, PAGE_SIZE, H_B, HD_B), 1.0),
        'cache_v': nrm(5, (n_phys, N_EVEN, PAGE_SIZE, H_B, HD_B), 1.0),
        'page_table': page_table,
        'state_lru_h': nrm(6, (DEC_BATCH, N_EVEN, W_A), 0.5),
        'state_lru_conv': nrm(7, (DEC_BATCH, N_EVEN, LRU_CONV - 1, W_A), 1.0),
        'state_pool': nrm(8, (DEC_BATCH, N_ODD, POOL_BUF, W_C), 1.0),
        'state_dconv': nrm(9, (DEC_BATCH, N_ODD, D_CONV - 1, W_D), 0.5),
        'norm_even': 1.0 + nrm(10, (N_EVEN, D_MODEL), 0.02),
        'w_in_even': nrm(11, (N_EVEN, D_MODEL, IN_EVEN), D_MODEL ** -0.5),
        'lru_conv_w': nrm(12, (N_EVEN, LRU_CONV, W_A), LRU_CONV ** -0.5),
        'lru_conv_b': nrm(13, (N_EVEN, W_A), 0.01),
        'lru_wr': nrm(14, (N_EVEN, H_A, BW_A, BW_A), BW_A ** -0.5),
        'lru_br': nrm(15, (N_EVEN, W_A), 0.01),
        'lru_wi': nrm(16, (N_EVEN, H_A, BW_A, BW_A), BW_A ** -0.5),
        'lru_bi': nrm(17, (N_EVEN, W_A), 0.01),
        'lru_lambda': lru_lambda,
        'w_out_even': nrm(18, (N_EVEN, MIX_EVEN, D_MODEL), MIX_EVEN ** -0.5),
        'norm_odd': 1.0 + nrm(19, (N_ODD, D_MODEL), 0.02),
        'w_in_odd': nrm(20, (N_ODD, D_MODEL, IN_ODD), D_MODEL ** -0.5),
        'pool_w': nrm(21, (N_ODD, N_POOL_GROUPS, GW_C, GW_C), GW_C ** -0.5),
        'pool_b': nrm(22, (N_ODD, W_C), 0.01),
        'pool_scale': 1.0 + nrm(23, (N_ODD, W_C), 0.02),
        'dconv_w': nrm(24, (N_ODD, D_CONV, W_D), D_CONV ** -0.5),
        'dconv_b': nrm(25, (N_ODD, W_D), 0.01),
        'dln_g': 1.0 + nrm(26, (N_ODD, W_D), 0.02),
        'dln_b': nrm(27, (N_ODD, W_D), 0.01),
        'w_out_odd': nrm(28, (N_ODD, MIX_ODD, D_MODEL), MIX_ODD ** -0.5),
        'norm_final': 1.0 + nrm(29, (D_MODEL,), 0.02),
    }


def reference(x_prompt, x_sample, cache_k, cache_v, page_table, state_lru_h, state_lru_conv, state_pool, state_dconv,
              norm_even, w_in_even, lru_conv_w, lru_conv_b, lru_wr, lru_br, lru_wi, lru_bi, lru_lambda, w_out_even,
              norm_odd, w_in_odd, pool_w, pool_b, pool_scale, dconv_w, dconv_b, dln_g, dln_b, w_out_odd, norm_final):
    slopes = alibi_slopes(H_B)
    n_dec, n_pages = page_table.shape
    past_len = n_pages * PAGE_SIZE
    bp = x_prompt.shape[0]
    xp, xs = x_prompt, x_sample
    kp_l, vp_l, ks_l, vs_l = [], [], [], []
    hp_l, hs_l, cp_l, cs_l = [], [], [], []
    pp_l, ps_l, dp_l, ds_l = [], [], [], []
    for layer in range(DEPTH):
        li = layer // 2
        if layer % 2 == 0:
            ew = (w_in_even[li], lru_conv_w[li], lru_conv_b[li], lru_wr[li], lru_br[li], lru_wi[li], lru_bi[li],
                  lru_lambda[li], w_out_even[li])
            attend_p = functools.partial(moba_prompt, slopes=slopes)
            yp, kp, vp, hp, cp = even_mixer(rmsnorm(xp, norm_even[li]), attend_p,
                                            jnp.zeros((bp, LRU_CONV - 1, W_A), xp.dtype),
                                            jnp.zeros((bp, W_A), xp.dtype), *ew)
            xp = xp + yp
            k_past = cache_k[page_table, li].reshape(n_dec, past_len, H_B, HD_B)
            v_past = cache_v[page_table, li].reshape(n_dec, past_len, H_B, HD_B)
            attend_s = functools.partial(moba_sample, k_past=k_past, v_past=v_past, slopes=slopes)
            ys, ks_, vs_, hs, cs = even_mixer(rmsnorm(xs, norm_even[li]), attend_s,
                                              state_lru_conv[:, li], state_lru_h[:, li], *ew)
            xs = xs + ys
            kp_l.append(kp); vp_l.append(vp); ks_l.append(ks_); vs_l.append(vs_)
            hp_l.append(hp); hs_l.append(hs); cp_l.append(cp); cs_l.append(cs)
        else:
            ow = (w_in_odd[li], pool_w[li], pool_b[li], pool_scale[li], dconv_w[li], dconv_b[li], dln_g[li],
                  dln_b[li], w_out_odd[li])
            yp, pp, dp = odd_mixer(rmsnorm(xp, norm_odd[li]), 0,
                                   jnp.zeros((bp, POOL_BUF, W_C), xp.dtype),
                                   jnp.zeros((bp, D_CONV - 1, W_D), xp.dtype), *ow)
            xp = xp + yp
            ys, ps, ds = odd_mixer(rmsnorm(xs, norm_odd[li]), past_len,
                                   state_pool[:, li], state_dconv[:, li], *ow)
            xs = xs + ys
            pp_l.append(pp); ps_l.append(ps); dp_l.append(dp); ds_l.append(ds)
    y_prompt = rmsnorm(xp, norm_final)
    y_sample = rmsnorm(xs, norm_final)
    new_k_prompt = jnp.stack(kp_l, axis=1)
    new_v_prompt = jnp.stack(vp_l, axis=1)
    new_k_sample = jnp.stack(ks_l, axis=1)
    new_v_sample = jnp.stack(vs_l, axis=1)
    lru_h_prompt = jnp.stack(hp_l, axis=1)
    lru_h_sample = jnp.stack(hs_l, axis=1)
    lru_conv_prompt = jnp.stack(cp_l, axis=1)
    lru_conv_sample = jnp.stack(cs_l, axis=1)
    pool_prompt = jnp.stack(pp_l, axis=1)
    pool_sample = jnp.stack(ps_l, axis=1)
    dconv_prompt = jnp.stack(dp_l, axis=1)
    dconv_sample = jnp.stack(ds_l, axis=1)
    return (y_prompt, y_sample, new_k_prompt, new_v_prompt, new_k_sample, new_v_sample,
            lru_h_prompt, lru_h_sample, lru_conv_prompt, lru_conv_sample,
            pool_prompt, pool_sample, dconv_prompt, dconv_sample)
```

```python
import functools

import jax
import jax.numpy as jnp
from jax import lax
from jax.experimental import pallas as pl
from jax.experimental.pallas import tpu as pltpu

_MOBA_BLOCK = 256
_MOBA_TOPK = 3
_N_HEADS = 16
_LRU_BLOCKS = 8
_LRU_C = 8.0
_POOL_WINDOWS = (2, 4, 8, 16)
_EPS = 1e-6
_NEG = -1e30

_LANES = 128
_SUBLANES = 8
_VMEM_LIMIT_BYTES = 56 * 1024 * 1024

_BF16 = jnp.bfloat16
_F32 = jnp.float32
_NT = (((1,), (1,)), ((), ()))


def _cparams(*sem):
    return pltpu.CompilerParams(dimension_semantics=sem, vmem_limit_bytes=_VMEM_LIMIT_BYTES)


def _silu(x):
    return x * jax.nn.sigmoid(x)


def _rms(x, g):
    return x * lax.rsqrt(jnp.mean(x * x, axis=-1, keepdims=True) + _EPS) * g


def _in_proj_kernel(x_ref, g_ref, w_ref, z_ref, h_ref):
    @pl.when(pl.program_id(1) == 0)
    def _():
        h_ref[...] = _rms(x_ref[...], g_ref[...]).astype(_BF16)

    z_ref[...] = jnp.dot(h_ref[...], w_ref[...], preferred_element_type=_F32)


def _in_proj(x, g, w_bf16, *, tm, tn):
    m, d = x.shape
    n = w_bf16.shape[1]
    return pl.pallas_call(
        _in_proj_kernel,
        grid=(m // tm, n // tn),
        in_specs=[
            pl.BlockSpec((tm, d), lambda i, j: (i, 0)),
            pl.BlockSpec((1, d), lambda i, j: (0, 0)),
            pl.BlockSpec((d, tn), lambda i, j: (0, j)),
        ],
        out_specs=pl.BlockSpec((tm, tn), lambda i, j: (i, j)),
        out_shape=jax.ShapeDtypeStruct((m, n), _F32),
        scratch_shapes=[pltpu.VMEM((tm, d), _BF16)],
        compiler_params=_cparams("parallel", "arbitrary"),
        name="in_proj",
    )(x, g.reshape(1, d), w_bf16)


def _topk_select(gate, blk_iota, nblk):
    sel = jnp.zeros(gate.shape, jnp.bool_)
    g = gate
    for _ in range(_MOBA_TOPK):
        mx = jnp.max(g, axis=1, keepdims=True)
        first = jnp.min(jnp.where(g == mx, blk_iota, nblk), axis=1, keepdims=True)
        pick = blk_iota == first
        sel = jnp.logical_or(sel, pick)
        g = jnp.where(pick, -jnp.inf, g)
    return sel


def _moba_prompt_kernel(slopes_ref, q_ref, k_ref, v_ref, o_ref,
                        kb_ref, vb_ref, km_ref, m_ref, l_ref, acc_ref, pen_ref, *, hd):
    blk = _MOBA_BLOCK
    hp = pl.program_id(1)
    qi = pl.program_id(2)
    nblk = km_ref.shape[0]
    width = q_ref.shape[2]

    @pl.when(qi == 0)
    def _():
        for j in range(nblk):
            kj = k_ref[0, j * blk:(j + 1) * blk, :]
            kb_ref[j * blk:(j + 1) * blk, :] = kj.astype(_BF16)
            vb_ref[j * blk:(j + 1) * blk, :] = v_ref[0, j * blk:(j + 1) * blk, :].astype(_BF16)
            km_ref[j:j + 1, :] = jnp.sum(kj, axis=0, keepdims=True) * (1.0 / blk)

    q = q_ref[0]
    lane = lax.broadcasted_iota(jnp.int32, (blk, width), 1)
    blk_iota = lax.broadcasted_iota(jnp.int32, (blk, nblk), 1)
    key_r = lax.broadcasted_iota(jnp.int32, (1, blk), 1).astype(_F32)
    row_i = lax.broadcasted_iota(jnp.int32, (blk, blk), 0)
    col_i = lax.broadcasted_iota(jnp.int32, (blk, blk), 1)
    past = blk_iota < qi
    q_start = pl.multiple_of(qi * blk, blk)

    outs = []
    for hh in range(width // hd):
        slope = slopes_ref[hp * (width // hd) + hh]
        qm = jnp.where(jnp.logical_and(lane >= hh * hd, lane < (hh + 1) * hd), q, 0.0)
        gate = lax.dot_general(qm, km_ref[...], _NT, precision=lax.Precision.HIGHEST,
                               preferred_element_type=_F32)
        sel = _topk_select(jnp.where(past, gate, _NEG), blk_iota, nblk)
        pen_ref[...] = jnp.where(jnp.logical_and(sel, past), 0.0, _NEG)

        qs = (qm * (hd ** -0.5)).astype(_BF16)
        kbias = slope * key_r

        s = lax.dot_general(qs, kb_ref[pl.ds(q_start, blk), :], _NT, preferred_element_type=_F32) + kbias
        s = jnp.where(col_i <= row_i, s, _NEG)
        m0 = jnp.max(s, axis=1, keepdims=True)
        p = jnp.exp(s - m0)
        m_ref[...] = m0
        l_ref[...] = jnp.sum(p, axis=1, keepdims=True)
        acc_ref[...] = jnp.dot(p.astype(_BF16), vb_ref[pl.ds(q_start, blk), :], preferred_element_type=_F32)

        def body(j, carry):
            start = pl.multiple_of(j * blk, blk)
            s = lax.dot_general(qs, kb_ref[pl.ds(start, blk), :], _NT, preferred_element_type=_F32) + kbias
            pen_j = jnp.sum(jnp.where(blk_iota == j, pen_ref[...], 0.0), axis=1, keepdims=True)
            off = pen_j - slope * ((qi - j) * blk).astype(_F32)
            m_old = m_ref[...]
            m_new = jnp.maximum(m_old, jnp.max(s, axis=1, keepdims=True) + off)
            alpha = jnp.exp(m_old - m_new)
            p = jnp.exp(s - (m_new - off))
            l_ref[...] = alpha * l_ref[...] + jnp.sum(p, axis=1, keepdims=True)
            acc_ref[...] = alpha * acc_ref[...] + jnp.dot(
                p.astype(_BF16), vb_ref[pl.ds(start, blk), :], preferred_element_type=_F32)
            m_ref[...] = m_new
            return carry

        lax.fori_loop(0, qi, body, 0)
        outs.append(acc_ref[...] / l_ref[...])

    o = outs[0]
    for hh in range(1, len(outs)):
        o = jnp.where(lane >= hh * hd, outs[hh], o)
    o_ref[0] = o


def _moba_prompt(z3, slopes, *, q_col, k_col, v_col, w_b, hd):
    b, s, _ = z3.shape
    width = _LANES
    nblk = s // _MOBA_BLOCK
    kern = functools.partial(_moba_prompt_kernel, hd=hd)
    grid_spec = pltpu.PrefetchScalarGridSpec(
        num_scalar_prefetch=1,
        grid=(b, w_b // width, nblk),
        in_specs=[
            pl.BlockSpec((1, _MOBA_BLOCK, width), lambda bi, hp, qi, sl: (bi, qi, q_col + hp)),
            pl.BlockSpec((1, s, width), lambda bi, hp, qi, sl: (bi, 0, k_col + hp)),
            pl.BlockSpec((1, s, width), lambda bi, hp, qi, sl: (bi, 0, v_col + hp)),
        ],
        out_specs=pl.BlockSpec((1, _MOBA_BLOCK, width), lambda bi, hp, qi, sl: (bi, qi, hp)),
        scratch_shapes=[
            pltpu.VMEM((s, width), _BF16),
            pltpu.VMEM((s, width), _BF16),
            pltpu.VMEM((nblk, width), _F32),
            pltpu.VMEM((_MOBA_BLOCK, 1), _F32),
            pltpu.VMEM((_MOBA_BLOCK, 1), _F32),
            pltpu.VMEM((_MOBA_BLOCK, width), _F32),
            pltpu.VMEM((_MOBA_BLOCK, nblk), _F32),
        ],
    )
    return pl.pallas_call(
        kern,
        grid_spec=grid_spec,
        out_shape=jax.ShapeDtypeStruct((b, s, w_b), _F32),
        compiler_params=_cparams("parallel", "parallel", "arbitrary"),
        name="moba_prompt",
    )(slopes, z3, z3, z3)


def _moba_sample_kernel(pt_ref, q_ref, kn_ref, vn_ref, k0_ref, k1_ref, v0_ref, v1_ref,
                        slope_ref, tpos_ref, o_ref,
                        qbd_ref, qf_ref, ksum_ref, m_ref, l_ref, acc_ref, *, hd, past_len):
    blk = _MOBA_BLOCK
    j = pl.program_id(1)
    nblk = pl.num_programs(1)
    n_t = q_ref.shape[0]
    w_b = q_ref.shape[3]
    n_h = w_b // hd
    rows = n_t * n_h
    nb = m_ref.shape[1]

    row_c = lax.broadcasted_iota(jnp.int32, (rows, w_b), 0)
    lane_c = lax.broadcasted_iota(jnp.int32, (rows, w_b), 1)
    head_mask = (row_c % n_h) == (lane_c // hd)
    slope = slope_ref[...]
    tpos = tpos_ref[...]

    @pl.when(j == 0)
    def _():
        q = q_ref[:, 0, 0, :]
        qrep = jnp.concatenate([jnp.broadcast_to(q[t:t + 1, :], (n_h, w_b)) for t in range(n_t)], axis=0)
        qf = jnp.where(head_mask, qrep, 0.0)
        qf_ref[...] = qf
        qbd_ref[...] = (qf * (hd ** -0.5)).astype(_BF16)
        m_ref[...] = jnp.zeros(m_ref.shape, _F32)
        l_ref[...] = jnp.zeros(l_ref.shape, _F32)

    k = jnp.concatenate([k0_ref[0, 0], k1_ref[0, 0]], axis=0)
    v = jnp.concatenate([v0_ref[0, 0], v1_ref[0, 0]], axis=0)
    ksum_ref[j] = jnp.sum(k.reshape(blk // _SUBLANES, _SUBLANES, w_b), axis=0)

    key_r = lax.broadcasted_iota(jnp.int32, (1, blk), 1).astype(_F32)
    s = lax.dot_general(qbd_ref[...], k.astype(_BF16), _NT, preferred_element_type=_F32) + slope * key_r
    m_loc = jnp.max(s, axis=1, keepdims=True)
    p = jnp.exp(s - m_loc)
    l_loc = jnp.sum(p, axis=1, keepdims=True)
    acc_ref[j] = jnp.dot(p.astype(_BF16), v.astype(_BF16), preferred_element_type=_F32)
    m_blk = m_loc - slope * (past_len + tpos - (j * blk).astype(_F32))
    blk_iota = lax.broadcasted_iota(jnp.int32, (rows, nb), 1)
    m_ref[...] = jnp.where(blk_iota == j, m_blk, m_ref[...])
    l_ref[...] = jnp.where(blk_iota == j, l_loc, l_ref[...])

    @pl.when(j == nblk - 1)
    def _():
        kmean = jnp.sum(ksum_ref[...], axis=1) * (1.0 / blk)
        gate = lax.dot_general(qf_ref[...], kmean, _NT, precision=lax.Precision.HIGHEST,
                               preferred_element_type=_F32)
        sel = _topk_select(gate, blk_iota, nb)

        pad = _SUBLANES
        kn = jnp.concatenate([kn_ref[:, 0, 0, :], jnp.zeros((pad - n_t, w_b), _F32)], axis=0)
        vn = vn_ref[:, 0, 0, :]
        r_own = lax.broadcasted_iota(jnp.int32, (rows, pad), 1).astype(_F32)
        s_own = lax.dot_general(qbd_ref[...], kn.astype(_BF16), _NT, preferred_element_type=_F32)
        s_own = jnp.where(r_own <= tpos, s_own - slope * (tpos - r_own), _NEG)
        m_all = m_ref[...]
        m_fin = jnp.maximum(jnp.max(s_own, axis=1, keepdims=True),
                            jnp.max(jnp.where(sel, m_all, -jnp.inf), axis=1, keepdims=True))
        w = jnp.where(sel, jnp.exp(m_all - m_fin), 0.0)
        p_own = jnp.exp(s_own - m_fin)
        denom = jnp.sum(w * l_ref[...], axis=1, keepdims=True) + jnp.sum(p_own, axis=1, keepdims=True)
        own_iota = lax.broadcasted_iota(jnp.int32, (rows, pad), 1)
        out = jnp.zeros((rows, w_b), _F32)
        for r in range(n_t):
            p_r = jnp.sum(jnp.where(own_iota == r, p_own, 0.0), axis=1, keepdims=True)
            out = out + p_r * vn[r:r + 1, :]

        def merge(jj, out):
            w_j = jnp.sum(jnp.where(blk_iota == jj, w, 0.0), axis=1, keepdims=True)
            return out + w_j * acc_ref[jj]

        out = lax.fori_loop(0, nb, merge, out)
        out = jnp.where(head_mask, out / denom, 0.0)
        o_ref[:, 0, 0, :] = jnp.sum(out.reshape(n_t, n_h, w_b), axis=1)


def _moba_sample(z4, cache_k4, cache_v4, page_table, slope_rows, tpos_rows, *, li, q_col, k_col, v_col,
                 w_b, hd, page_size):
    n_t, n_seq = z4.shape[0], z4.shape[1]
    n_pages = page_table.shape[1]
    pages_per_blk = _MOBA_BLOCK // page_size
    assert pages_per_blk == 2
    nb = n_pages // pages_per_blk
    rows = n_t * (w_b // hd)
    kern = functools.partial(_moba_sample_kernel, hd=hd, past_len=float(n_pages * page_size))
    tok = lambda col: pl.BlockSpec((n_t, 1, 1, w_b), lambda n, j, pt: (0, n, 0, col))
    page = lambda half: pl.BlockSpec((1, 1, page_size, w_b), lambda n, j, pt: (pt[n, 2 * j + half], li, 0, 0))
    const = pl.BlockSpec((rows, 1), lambda n, j, pt: (0, 0))
    grid_spec = pltpu.PrefetchScalarGridSpec(
        num_scalar_prefetch=1,
        grid=(n_seq, nb),
        in_specs=[tok(q_col), tok(k_col), tok(v_col), page(0), page(1), page(0), page(1), const, const],
        out_specs=pl.BlockSpec((n_t, 1, 1, w_b), lambda n, j, pt: (0, n, 0, 0)),
        scratch_shapes=[
            pltpu.VMEM((rows, w_b), _BF16),
            pltpu.VMEM((rows, w_b), _F32),
            pltpu.VMEM((nb, _SUBLANES, w_b), _F32),
            pltpu.VMEM((rows, nb), _F32),
            pltpu.VMEM((rows, nb), _F32),
            pltpu.VMEM((nb, rows, w_b), _F32),
        ],
    )
    return pl.pallas_call(
        kern,
        grid_spec=grid_spec,
        out_shape=jax.ShapeDtypeStruct((n_t, n_seq, 1, w_b), _F32),
        compiler_params=_cparams("parallel", "arbitrary"),
        name="moba_sample",
    )(page_table, z4, z4, z4, cache_k4, cache_k4, cache_v4, cache_v4, slope_rows, tpos_rows)


def _lru_gates(xc, wr_ref, br, wi_ref, bi, lam):
    bw = wr_ref.shape[1]
    xb = xc.astype(_BF16)
    r_parts, i_parts = [], []
    for g in range(wr_ref.shape[0]):
        xg = xb[:, g * bw:(g + 1) * bw]
        r_parts.append(jnp.dot(xg, wr_ref[g], preferred_element_type=_F32))
        i_parts.append(jnp.dot(xg, wi_ref[g], preferred_element_type=_F32))
    r = jax.nn.sigmoid(jnp.concatenate(r_parts, axis=1) + br)
    i = jax.nn.sigmoid(jnp.concatenate(i_parts, axis=1) + bi)
    softplus_neg_lam = jnp.maximum(-lam, 0.0) + jnp.log1p(jnp.exp(-jnp.abs(lam)))
    log_a = (-_LRU_C * softplus_neg_lam) * r
    a = jnp.exp(log_a)
    bx = jnp.sqrt(1.0 - a * a) * (i * xc)
    return a, bx


def _scan_rows(a, b):
    n = a.shape[0]
    row = lax.broadcasted_iota(jnp.int32, a.shape, 0)
    s = 1
    while s < n:
        keep = row >= s
        a_sh = jnp.where(keep, pltpu.roll(a, s, 0), 1.0)
        b_sh = jnp.where(keep, pltpu.roll(b, s, 0), 0.0)
        b = a * b_sh + b
        a = a * a_sh
        s *= 2
    return a, b


def _out_proj(mix_a, mix_b, wout_ref, x):
    half = wout_ref.shape[0] // 2
    y = jnp.dot(mix_a.astype(_BF16), wout_ref[0:half, :], preferred_element_type=_F32)
    y = y + jnp.dot(mix_b.astype(_BF16), wout_ref[half:2 * half, :], preferred_element_type=_F32)
    return x + y


def _even_out_prompt_kernel(xa_ref, ga_ref, gb_ref, o_ref, x_ref, cw_ref, cb_ref, wr_ref, br_ref, wi_ref,
                            bi_ref, lam_ref, wout_ref, y_ref, hlast_ref, ext_ref, h_ref):
    tile = xa_ref.shape[1]
    pre = _SUBLANES
    taps = cw_ref.shape[0]

    @pl.when(pl.program_id(1) == 0)
    def _():
        ext_ref[0:pre, :] = jnp.zeros((pre, ext_ref.shape[1]), _F32)
        h_ref[...] = jnp.zeros(h_ref.shape, _F32)

    xa = xa_ref[0]
    ext_ref[pre:pre + tile, :] = xa
    xc = cb_ref[...] + cw_ref[taps - 1:taps, :] * xa
    for k in range(taps - 1):
        off = pre - (taps - 1) + k
        xc = xc + cw_ref[k:k + 1, :] * ext_ref[off:off + tile, :]
    ext_ref[0:pre, :] = xa[tile - pre:tile, :]

    a, bx = _lru_gates(xc, wr_ref, br_ref[...], wi_ref, bi_ref[...], lam_ref[...])
    a_cum, h = _scan_rows(a, bx)
    h = h + a_cum * h_ref[...]
    h_ref[...] = h[tile - 1:tile, :]
    hlast_ref[0] = h[tile - 1:tile, :]

    y_ref[0] = _out_proj(h * _silu(ga_ref[0]), o_ref[0] * _silu(gb_ref[0]), wout_ref, x_ref[0])


def _even_out_prompt(z3, o, x, cw, cb, wr, br, wi, bi, lam, wout, *, tile, w_a):
    b, s, d = x.shape
    nc = w_a // d
    assert nc == 1
    row = lambda col: pl.BlockSpec((1, tile, d), lambda bi_, t: (bi_, t, col))
    full = lambda arr: pl.BlockSpec(arr.shape, lambda bi_, t: (0,) * arr.ndim)
    vec = pl.BlockSpec((1, d), lambda bi_, t: (0, 0))
    return pl.pallas_call(
        _even_out_prompt_kernel,
        grid=(b, s // tile),
        in_specs=[row(0), row(1), row(5), row(0), row(0), full(cw), vec, full(wr), vec, full(wi), vec, vec,
                  full(wout)],
        out_specs=[row(0), pl.BlockSpec((1, 1, d), lambda bi_, t: (bi_, 0, 0))],
        out_shape=[jax.ShapeDtypeStruct((b, s, d), _F32), jax.ShapeDtypeStruct((b, 1, d), _F32)],
        scratch_shapes=[pltpu.VMEM((_SUBLANES + tile, d), _F32), pltpu.VMEM((1, d), _F32)],
        compiler_params=_cparams("parallel", "arbitrary"),
        name="even_out_prompt",
    )(z3, z3, z3, o, x, cw, cb.reshape(1, d), wr, br.reshape(1, d), wi, bi.reshape(1, d), lam.reshape(1, d), wout)


def _even_out_sample_kernel(z_ref, o_ref, x_ref, pre_ref, h0_ref, cw_ref, cb_ref, wr_ref, br_ref, wi_ref,
                            bi_ref, lam_ref, wout_ref, y_ref, hlast_ref, cstate_ref, *, n_seq):
    d = x_ref.shape[1]
    rows = x_ref.shape[0]
    taps = cw_ref.shape[0]
    xa = z_ref[:, 0:d]
    ext = jnp.concatenate([pre_ref[...], xa], axis=0)
    xc = cb_ref[...]
    for k in range(taps):
        xc = xc + cw_ref[k:k + 1, :] * ext[k * n_seq:k * n_seq + rows, :]
    cstate_ref[...] = ext[rows:rows + (taps - 1) * n_seq, :]

    a, bx = _lru_gates(xc, wr_ref, br_ref[...], wi_ref, bi_ref[...], lam_ref[...])
    h = h0_ref[...]
    hs = []
    for t in range(rows // n_seq):
        h = a[t * n_seq:(t + 1) * n_seq, :] * h + bx[t * n_seq:(t + 1) * n_seq, :]
        hs.append(h)
    hlast_ref[...] = h
    ha = jnp.concatenate(hs, axis=0)
    y_ref[...] = _out_proj(ha * _silu(z_ref[:, d:2 * d]), o_ref[...] * _silu(z_ref[:, 5 * d:6 * d]), wout_ref,
                           x_ref[...])


def _even_out_sample(z, o, x, pre, h0, cw, cb, wr, br, wi, bi, lam, wout, *, n_seq):
    rows, d = x.shape
    args = (z, o, x, pre, h0, cw, cb.reshape(1, d), wr, br.reshape(1, d), wi, bi.reshape(1, d), lam.reshape(1, d),
            wout)
    full = lambda arr: pl.BlockSpec(arr.shape, lambda i: (0,) * arr.ndim)
    outs = [jax.ShapeDtypeStruct((rows, d), _F32), jax.ShapeDtypeStruct((n_seq, d), _F32),
            jax.ShapeDtypeStruct(pre.shape, _F32)]
    return pl.pallas_call(
        functools.partial(_even_out_sample_kernel, n_seq=n_seq),
        grid=(1,),
        in_specs=[full(a) for a in args],
        out_specs=[full(o_) for o_ in outs],
        out_shape=outs,
        compiler_params=_cparams("arbitrary"),
        name="even_out_sample",
    )(*args)


def _pool_project(pooled, pw_ref, pb, ps):
    gw = pw_ref.shape[1]
    pb16 = pooled.astype(_BF16)
    parts = [jnp.dot(pb16[:, g * gw:(g + 1) * gw], pw_ref[g], preferred_element_type=_F32)
             for g in range(pw_ref.shape[0])]
    return (jnp.concatenate(parts, axis=1) + pb) * ps


def _layernorm(x, g, b):
    xc = x - jnp.mean(x, axis=-1, keepdims=True)
    var = jnp.mean(xc * xc, axis=-1, keepdims=True)
    return xc * lax.rsqrt(var + _EPS) * g + b


def _odd_out_prompt_kernel(xc_ref, gc_ref, da_ref, db_ref, gd_ref, x_ref, pw_ref, pb_ref, ps_ref, dw_ref,
                           dbias_ref, lg_ref, lb_ref, wout_ref, nf_ref, y_ref, dstate_ref, pext_ref, uext_ref,
                           *, final_norm):
    tile = xc_ref.shape[1]
    d = xc_ref.shape[2]
    ppre = pext_ref.shape[0] - tile
    upre = uext_ref.shape[0] - tile
    taps = dw_ref.shape[0]
    t_idx = pl.program_id(1)

    @pl.when(t_idx == 0)
    def _():
        pext_ref[0:ppre, :] = jnp.zeros((ppre, d), _F32)
        uext_ref[0:upre, :] = jnp.zeros((upre, d), _F32)

    xc = xc_ref[0]
    pext_ref[ppre:ppre + tile, :] = xc
    pos = t_idx * tile + lax.broadcasted_iota(jnp.int32, (tile, 1), 0)
    gw = d // len(_POOL_WINDOWS)
    parts = []
    for g, win in enumerate(_POOL_WINDOWS):
        ssum = xc[:, g * gw:(g + 1) * gw]
        for sft in range(1, win):
            ssum = ssum + pext_ref[ppre - sft:ppre - sft + tile, g * gw:(g + 1) * gw]
        cnt = jnp.minimum(win, pos + 1).astype(_F32)
        parts.append(ssum / cnt)
    pooled = jnp.concatenate(parts, axis=1) - xc
    pext_ref[0:ppre, :] = xc[tile - ppre:tile, :]
    yc = _pool_project(pooled, pw_ref, pb_ref[...], ps_ref[...])

    u = da_ref[0] * jax.nn.sigmoid(db_ref[0])
    uext_ref[upre:upre + tile, :] = u
    dc = dbias_ref[...] + dw_ref[taps - 1:taps, :] * u
    for k in range(taps - 1):
        off = upre - (taps - 1) + k
        dc = dc + dw_ref[k:k + 1, :] * uext_ref[off:off + tile, :]
    uext_ref[0:upre, :] = u[tile - upre:tile, :]
    dstate_ref[0] = u[tile - upre:tile, :]
    yd = _silu(_layernorm(dc, lg_ref[...], lb_ref[...]))

    y = _out_proj(yc * _silu(gc_ref[0]), yd * _silu(gd_ref[0]), wout_ref, x_ref[0])
    if final_norm:
        y = _rms(y, nf_ref[...])
    y_ref[0] = y


def _odd_out_prompt(z3, x, pw, pb, ps, dw, dbias, lg, lb, wout, nf, *, tile, final_norm):
    b, s, d = x.shape
    ppre = 16
    upre = 32
    assert max(_POOL_WINDOWS) - 1 <= ppre and dw.shape[0] - 1 <= upre
    row = lambda col: pl.BlockSpec((1, tile, d), lambda bi_, t: (bi_, t, col))
    full = lambda arr: pl.BlockSpec(arr.shape, lambda bi_, t: (0,) * arr.ndim)
    vec = pl.BlockSpec((1, d), lambda bi_, t: (0, 0))
    return pl.pallas_call(
        functools.partial(_odd_out_prompt_kernel, final_norm=final_norm),
        grid=(b, s // tile),
        in_specs=[row(0), row(1), row(2), row(3), row(4), row(0), full(pw), vec, vec, full(dw), vec, vec, vec,
                  full(wout), vec],
        out_specs=[row(0), pl.BlockSpec((1, upre, d), lambda bi_, t: (bi_, 0, 0))],
        out_shape=[jax.ShapeDtypeStruct((b, s, d), _F32), jax.ShapeDtypeStruct((b, upre, d), _F32)],
        scratch_shapes=[pltpu.VMEM((ppre + tile, d), _F32), pltpu.VMEM((upre + tile, d), _F32)],
        compiler_params=_cparams("parallel", "arbitrary"),
        name="odd_out_prompt",
    )(z3, z3, z3, z3, z3, x, pw, pb.reshape(1, d), ps.reshape(1, d), dw, dbias.reshape(1, d), lg.reshape(1, d),
      lb.reshape(1, d), wout, nf.reshape(1, d))


def _odd_out_sample_kernel(z_ref, x_ref, ppre_ref, upre_ref, pw_ref, pb_ref, ps_ref, dw_ref, dbias_ref, lg_ref,
                           lb_ref, wout_ref, nf_ref, y_ref, pstate_ref, dstate_ref, *, n_seq, pos0, final_norm):
    rows, d = x_ref.shape
    taps = dw_ref.shape[0]
    pbuf = ppre_ref.shape[0] // n_seq
    xc = z_ref[:, 0:d]
    pext = jnp.concatenate([ppre_ref[...], xc], axis=0)
    pstate_ref[...] = pext[rows:rows + pbuf * n_seq, :]
    t_of_row = lax.broadcasted_iota(jnp.int32, (rows, 1), 0) // n_seq
    gw = d // len(_POOL_WINDOWS)
    parts = []
    for g, win in enumerate(_POOL_WINDOWS):
        ssum = xc[:, g * gw:(g + 1) * gw]
        for sft in range(1, win):
            lo = (pbuf - sft) * n_seq
            ssum = ssum + pext[lo:lo + rows, g * gw:(g + 1) * gw]
        cnt = jnp.minimum(win, pos0 + t_of_row + 1).astype(_F32)
        parts.append(ssum / cnt)
    pooled = jnp.concatenate(parts, axis=1) - xc
    yc = _pool_project(pooled, pw_ref, pb_ref[...], ps_ref[...])

    u = z_ref[:, 2 * d:3 * d] * jax.nn.sigmoid(z_ref[:, 3 * d:4 * d])
    uext = jnp.concatenate([upre_ref[...], u], axis=0)
    dstate_ref[...] = uext[rows:rows + (taps - 1) * n_seq, :]
    dc = dbias_ref[...]
    for k in range(taps):
        dc = dc + dw_ref[k:k + 1, :] * uext[k * n_seq:k * n_seq + rows, :]
    yd = _silu(_layernorm(dc, lg_ref[...], lb_ref[...]))

    y = _out_proj(yc * _silu(z_ref[:, d:2 * d]), yd * _silu(z_ref[:, 4 * d:5 * d]), wout_ref, x_ref[...])
    if final_norm:
        y = _rms(y, nf_ref[...])
    y_ref[...] = y


def _odd_out_sample(z, x, ppre, upre, pw, pb, ps, dw, dbias, lg, lb, wout, nf, *, n_seq, pos0, final_norm):
    rows, d = x.shape
    args = (z, x, ppre, upre, pw, pb.reshape(1, d), ps.reshape(1, d), dw, dbias.reshape(1, d), lg.reshape(1, d),
            lb.reshape(1, d), wout, nf.reshape(1, d))
    full = lambda arr: pl.BlockSpec(arr.shape, lambda i: (0,) * arr.ndim)
    outs = [jax.ShapeDtypeStruct((rows, d), _F32), jax.ShapeDtypeStruct(ppre.shape, _F32),
            jax.ShapeDtypeStruct(upre.shape, _F32)]
    return pl.pallas_call(
        functools.partial(_odd_out_sample_kernel, n_seq=n_seq, pos0=pos0, final_norm=final_norm),
        grid=(1,),
        in_specs=[full(a) for a in args],
        out_specs=[full(o_) for o_ in outs],
        out_shape=outs,
        compiler_params=_cparams("arbitrary"),
        name="odd_out_sample",
    )(*args)


def _time_major(state):
    n, k, c = state.shape
    return jnp.transpose(state, (1, 0, 2)).reshape(k * n, c)


def _batch_major(state_tm, n):
    kn, c = state_tm.shape
    return jnp.transpose(state_tm.reshape(kn // n, n, c), (1, 0, 2))


def kernel(x_prompt, x_sample, cache_k, cache_v, page_table, state_lru_h, state_lru_conv, state_pool, state_dconv, norm_even, w_in_even, lru_conv_w, lru_conv_b, lru_wr, lru_br, lru_wi, lru_bi, lru_lambda, w_out_even, norm_odd, w_in_odd, pool_w, pool_b, pool_scale, dconv_w, dconv_b, dln_g, dln_b, w_out_odd, norm_final):
    bp, seq, d = x_prompt.shape
    n_seq, n_t, _ = x_sample.shape
    n_even, n_odd = w_in_even.shape[0], w_in_odd.shape[0]
    depth = n_even + n_odd
    n_phys, _, page_size, n_heads, hd = cache_k.shape
    w_b = n_heads * hd
    n_pages = page_table.shape[1]
    past_len = n_pages * page_size
    lru_taps = lru_conv_w.shape[1]
    pool_buf = state_pool.shape[2]
    dconv_taps = dconv_w.shape[1]
    assert w_b == d and seq % _MOBA_BLOCK == 0 and past_len % _MOBA_BLOCK == 0 and n_t <= _SUBLANES
    assert seq // _MOBA_BLOCK >= _MOBA_TOPK and past_len // _MOBA_BLOCK >= _MOBA_TOPK
    cu = d // _LANES

    slopes = 2.0 ** (-8.0 * (jnp.arange(n_heads, dtype=_F32) + 1.0) / n_heads)
    slope_rows = jnp.tile(slopes, n_t).reshape(n_t * n_heads, 1)
    tpos_rows = jnp.repeat(jnp.arange(n_t, dtype=_F32), n_heads).reshape(n_t * n_heads, 1)
    cache_k4 = cache_k.reshape(n_phys, n_even, page_size, w_b)
    cache_v4 = cache_v.reshape(n_phys, n_even, page_size, w_b)

    xp = x_prompt
    xs = jnp.transpose(x_sample, (1, 0, 2)).reshape(n_t * n_seq, d)
    tile_p = 256
    tm_p = 1024

    kp_l, vp_l, ks_l, vs_l = [], [], [], []
    hp_l, hs_l, cp_l, cs_l = [], [], [], []
    pp_l, ps_l, dp_l, ds_l = [], [], [], []
    for layer in range(depth):
        li = layer // 2
        last = layer == depth - 1
        if layer % 2 == 0:
            w_in = w_in_even[li].astype(_BF16)
            wr, wi = lru_wr[li].astype(_BF16), lru_wi[li].astype(_BF16)
            wout = w_out_even[li].astype(_BF16)
            lru_w = (lru_conv_w[li], lru_conv_b[li], wr, lru_br[li], wi, lru_bi[li], lru_lambda[li], wout)
            z = _in_proj(xp.reshape(bp * seq, d), norm_even[li], w_in, tm=tm_p, tn=512)
            z3 = z.reshape(bp, seq, z.shape[1])
            o = _moba_prompt(z3, slopes, q_col=2 * cu, k_col=3 * cu, v_col=4 * cu, w_b=w_b, hd=hd)
            xp, h_last = _even_out_prompt(z3, o, xp, *lru_w, tile=tile_p, w_a=d)
            kp_l.append(z3[:, :, 3 * d:4 * d].reshape(bp, seq, n_heads, hd))
            vp_l.append(z3[:, :, 4 * d:5 * d].reshape(bp, seq, n_heads, hd))
            hp_l.append(h_last.reshape(bp, d))
            cp_l.append(z3[:, seq - (lru_taps - 1):, 0:d])
            zs = _in_proj(xs, norm_even[li], w_in, tm=n_t * n_seq, tn=512)
            zs4 = zs.reshape(n_t, n_seq, 1, zs.shape[1])
            os_ = _moba_sample(zs4, cache_k4, cache_v4, page_table, slope_rows, tpos_rows, li=li, q_col=2, k_col=3,
                               v_col=4, w_b=w_b, hd=hd, page_size=page_size)
            xs, hs_last, cs_tm = _even_out_sample(zs, os_.reshape(n_t * n_seq, w_b), xs,
                                                  _time_major(state_lru_conv[:, li]), state_lru_h[:, li], *lru_w,
                                                  n_seq=n_seq)
            zs3 = zs.reshape(n_t, n_seq, zs.shape[1])
            ks_l.append(jnp.transpose(zs3[:, :, 3 * d:4 * d], (1, 0, 2)).reshape(n_seq, n_t, n_heads, hd))
            vs_l.append(jnp.transpose(zs3[:, :, 4 * d:5 * d], (1, 0, 2)).reshape(n_seq, n_t, n_heads, hd))
            hs_l.append(hs_last)
            cs_l.append(_batch_major(cs_tm, n_seq))
        else:
            w_in = w_in_odd[li].astype(_BF16)
            odd_w = (pool_w[li].astype(_BF16), pool_b[li], pool_scale[li], dconv_w[li], dconv_b[li], dln_g[li],
                     dln_b[li], w_out_odd[li].astype(_BF16), norm_final)
            z = _in_proj(xp.reshape(bp * seq, d), norm_odd[li], w_in, tm=tm_p, tn=512)
            z3 = z.reshape(bp, seq, z.shape[1])
            xp, dstate = _odd_out_prompt(z3, xp, *odd_w, tile=tile_p, final_norm=last)
            pp_l.append(z3[:, seq - pool_buf:, 0:d])
            dp_l.append(dstate[:, dstate.shape[1] - (dconv_taps - 1):, :])
            zs = _in_proj(xs, norm_odd[li], w_in, tm=n_t * n_seq, tn=512)
            xs, ps_tm, ds_tm = _odd_out_sample(zs, xs, _time_major(state_pool[:, li]),
                                               _time_major(state_dconv[:, li]), *odd_w, n_seq=n_seq, pos0=past_len,
                                               final_norm=last)
            ps_l.append(_batch_major(ps_tm, n_seq))
            ds_l.append(_batch_major(ds_tm, n_seq))

    if depth % 2 == 1:
        raise NotImplementedError("final norm is fused into the last odd layer")
    y_prompt = xp
    y_sample = jnp.transpose(xs.reshape(n_t, n_seq, d), (1, 0, 2))
    stack = lambda xs_: jnp.stack(xs_, axis=1)
    return (y_prompt, y_sample, stack(kp_l), stack(vp_l), stack(ks_l), stack(vs_l),
            stack(hp_l), stack(hs_l), stack(cp_l), stack(cs_l),
            stack(pp_l), stack(ps_l), stack(dp_l), stack(ds_l))
```

```python
import functools

import jax
import jax.numpy as jnp
from jax import lax
from jax.experimental import pallas as pl
from jax.experimental.pallas import tpu as pltpu

_MOBA_BLOCK = 256
_MOBA_TOPK = 3
_MOBA_HEAD_GROUP = 4
_N_HEADS = 16
_LRU_BLOCKS = 8
_LRU_C = 8.0
_POOL_WINDOWS = (2, 4, 8, 16)
_EPS = 1e-6
_NEG = -1e30

_LANES = 128
_SUBLANES = 8
_BF16_ROWS = 16
_LOG2E = 1.4426950408889634
_VMEM_LIMIT_BYTES = 56 * 1024 * 1024

_BF16 = jnp.bfloat16
_F32 = jnp.float32
_NT = (((1,), (1,)), ((), ()))


def _cparams(*sem):
    return pltpu.CompilerParams(dimension_semantics=sem, vmem_limit_bytes=_VMEM_LIMIT_BYTES)


def _mxu(x, w):
    if w.dtype == _BF16:
        return jnp.dot(x.astype(_BF16), w, preferred_element_type=_F32)
    return jnp.dot(x, w, precision=lax.Precision.HIGHEST, preferred_element_type=_F32)


def _silu(x):
    return x * jax.nn.sigmoid(x)


def _rms(x, g):
    return x * lax.rsqrt(jnp.mean(x * x, axis=-1, keepdims=True) + _EPS) * g


def _in_proj_kernel(x_ref, g_ref, w_ref, z_ref, h_ref):
    @pl.when(pl.program_id(1) == 0)
    def _():
        h_ref[...] = _rms(x_ref[...], g_ref[...]).astype(h_ref.dtype)

    z_ref[...] = _mxu(h_ref[...], w_ref[...])


def _in_proj(x, g, w, *, tm, tn):
    m, d = x.shape
    n = w.shape[1]
    assert m % tm == 0 and n % tn == 0
    return pl.pallas_call(
        _in_proj_kernel,
        grid=(m // tm, n // tn),
        in_specs=[
            pl.BlockSpec((tm, d), lambda i, j: (i, 0)),
            pl.BlockSpec((1, d), lambda i, j: (0, 0)),
            pl.BlockSpec((d, tn), lambda i, j: (0, j)),
        ],
        out_specs=pl.BlockSpec((tm, tn), lambda i, j: (i, j)),
        out_shape=jax.ShapeDtypeStruct((m, n), _F32),
        scratch_shapes=[pltpu.VMEM((tm, d), w.dtype)],
        compiler_params=_cparams("parallel", "arbitrary"),
        name="in_proj",
    )(x, g.reshape(1, d), w)


def _topk_select(gate, blk_iota, nblk, axis):
    sel = jnp.zeros(gate.shape, jnp.bool_)
    g = gate
    for _ in range(_MOBA_TOPK):
        mx = jnp.max(g, axis=axis, keepdims=True)
        first = jnp.min(jnp.where(g == mx, blk_iota, nblk), axis=axis, keepdims=True)
        pick = blk_iota == first
        sel = jnp.logical_or(sel, pick)
        g = jnp.where(pick, -jnp.inf, g)
    return sel


def _moba_prompt_kernel(slopes_ref, q_ref, k_ref, v_ref, o_ref,
                        kb_ref, vt_ref, km_ref, kbias_ref, qb_ref, pen_ref, acc_ref, *, hd):
    blk = _MOBA_BLOCK
    hp = pl.program_id(1)
    qi = pl.program_id(2)
    nblk = km_ref.shape[0]
    width = q_ref.shape[2]
    n_h = width // hd

    va = vt_ref.shape[2]

    @pl.when(qi == 0)
    def _():
        ones_rows = (lax.broadcasted_iota(jnp.int32, (va - hd, blk), 0) == 0).astype(_F32)
        for j in range(nblk):
            kj = k_ref[0, j * blk:(j + 1) * blk, :]
            kb_ref[j * blk:(j + 1) * blk, :] = kj.astype(_BF16)
            km_ref[j:j + 1, :] = jnp.sum(kj, axis=0, keepdims=True) * (1.0 / blk)
            vt = v_ref[0, j * blk:(j + 1) * blk, :].T
            for hh in range(n_h):
                vt_ref[j, hh] = jnp.concatenate([vt[hh * hd:(hh + 1) * hd, :], ones_rows], axis=0).astype(_BF16)
        key_r = lax.broadcasted_iota(jnp.int32, (blk, blk), 0).astype(_F32)
        for hh in range(n_h):
            kbias_ref[hh] = (slopes_ref[hp * n_h + hh] * _LOG2E) * key_r

    qt = q_ref[0].T
    d_iota = lax.broadcasted_iota(jnp.int32, (width, blk), 0)
    blk_iota = lax.broadcasted_iota(jnp.int32, (nblk, blk), 0)
    key_i = lax.broadcasted_iota(jnp.int32, (blk, blk), 0)
    qry_i = lax.broadcasted_iota(jnp.int32, (blk, blk), 1)
    past = blk_iota < qi
    q_start = pl.multiple_of(qi * blk, blk)
    slopes = [slopes_ref[hp * n_h + hh] * _LOG2E for hh in range(n_h)]

    m_init = []
    for hh in range(n_h):
        qm = jnp.where(jnp.logical_and(d_iota >= hh * hd, d_iota < (hh + 1) * hd), qt, 0.0)
        gate = jnp.dot(km_ref[...], qm, precision=lax.Precision.HIGHEST, preferred_element_type=_F32)
        sel = _topk_select(jnp.where(past, gate, _NEG), blk_iota, nblk, axis=0)
        pen_ref[hh] = jnp.where(jnp.logical_and(sel, past), 0.0, _NEG)
        qb = (qm * (hd ** -0.5 * _LOG2E)).astype(_BF16)
        qb_ref[hh] = qb
        s = jnp.dot(kb_ref[pl.ds(q_start, blk), :], qb, preferred_element_type=_F32) + kbias_ref[hh]
        s = jnp.where(key_i <= qry_i, s, _NEG)
        m0 = jnp.max(s, axis=0, keepdims=True)
        m_init.append(m0)
        acc_ref[hh] = jnp.dot(vt_ref[qi, hh], jnp.exp2(s - m0).astype(_BF16), preferred_element_type=_F32)

    def body(jj, ms):
        j0 = 2 * jj
        kpair = kb_ref[pl.ds(pl.multiple_of(j0 * blk, 2 * blk), 2 * blk), :]
        s2 = [jnp.dot(kpair, qb_ref[hh], preferred_element_type=_F32) for hh in range(n_h)]
        new_m = []
        for hh in range(n_h):
            pen = pen_ref[hh]
            ts, offs = [], []
            for u in range(2):
                pen_u = jnp.sum(jnp.where(blk_iota == j0 + u, pen, 0.0), axis=0, keepdims=True)
                offs.append(pen_u - slopes[hh] * ((qi - j0 - u) * blk).astype(_F32))
                ts.append(s2[hh][u * blk:(u + 1) * blk, :] + kbias_ref[hh])
            m_loc = jnp.maximum(jnp.max(ts[0], axis=0, keepdims=True) + offs[0],
                                jnp.max(ts[1], axis=0, keepdims=True) + offs[1])
            m_new = jnp.maximum(ms[hh], m_loc)
            alpha = jnp.exp2(ms[hh] - m_new)
            pv = jnp.dot(vt_ref[j0, hh], jnp.exp2(ts[0] - (m_new - offs[0])).astype(_BF16),
                         preferred_element_type=_F32)
            pv = pv + jnp.dot(vt_ref[j0 + 1, hh], jnp.exp2(ts[1] - (m_new - offs[1])).astype(_BF16),
                              preferred_element_type=_F32)
            acc_ref[hh] = alpha * acc_ref[hh] + pv
            new_m.append(m_new)
        return tuple(new_m)

    lax.fori_loop(0, (qi + 1) // 2, body, tuple(m_init))
    outs = []
    for hh in range(n_h):
        acc = acc_ref[hh]
        outs.append(acc[0:hd, :] / acc[hd:hd + 1, :])
    o_ref[0] = jnp.concatenate(outs, axis=0).T


def _moba_prompt(z3, slopes, *, q_col, k_col, v_col, w_b, hd):
    b, s, _ = z3.shape
    width = _MOBA_HEAD_GROUP * hd
    n_h = width // hd
    groups = w_b // width
    nblk = s // _MOBA_BLOCK
    kern = functools.partial(_moba_prompt_kernel, hd=hd)
    resident = lambda col: pl.BlockSpec((1, s, width), lambda bi, hp, qi, sl: (bi, 0, col * groups + hp),
                                        pipeline_mode=pl.Buffered(1))
    grid_spec = pltpu.PrefetchScalarGridSpec(
        num_scalar_prefetch=1,
        grid=(b, groups, nblk),
        in_specs=[
            pl.BlockSpec((1, _MOBA_BLOCK, width), lambda bi, hp, qi, sl: (bi, qi, q_col * groups + hp)),
            resident(k_col),
            resident(v_col),
        ],
        out_specs=pl.BlockSpec((1, _MOBA_BLOCK, width), lambda bi, hp, qi, sl: (bi, qi, hp)),
        scratch_shapes=[
            pltpu.VMEM((s, width), _BF16),
            pltpu.VMEM((nblk, n_h, hd + _BF16_ROWS, _MOBA_BLOCK), _BF16),
            pltpu.VMEM((nblk, width), _F32),
            pltpu.VMEM((n_h, _MOBA_BLOCK, _MOBA_BLOCK), _F32),
            pltpu.VMEM((n_h, width, _MOBA_BLOCK), _BF16),
            pltpu.VMEM((n_h, nblk, _MOBA_BLOCK), _F32),
            pltpu.VMEM((n_h, hd + _BF16_ROWS, _MOBA_BLOCK), _F32),
        ],
    )
    return pl.pallas_call(
        kern,
        grid_spec=grid_spec,
        out_shape=jax.ShapeDtypeStruct((b, s, w_b), _F32),
        compiler_params=_cparams("parallel", "parallel", "arbitrary"),
        name="moba_prompt",
    )(slopes, z3, z3, z3)


def _moba_sample_kernel(pt_ref, q_ref, kn_ref, vn_ref, k0_ref, k1_ref, v0_ref, v1_ref,
                        bias_ref, bown_ref, slope_ref, tpos_ref, o_ref,
                        qb_ref, ksum_ref, m_ref, l_ref, acc_ref, *, past_len):
    blk = _MOBA_BLOCK
    j = pl.program_id(1)
    nblk = pl.num_programs(1)
    rows, hd = q_ref.shape[1], q_ref.shape[2]
    page, n_h = k0_ref.shape[2], k0_ref.shape[3]
    n_t = rows // n_h
    nb = m_ref.shape[1]
    slope = slope_ref[...]
    tpos = tpos_ref[...]
    blk_iota = lax.broadcasted_iota(jnp.int32, (rows, nb), 1)

    @pl.when(j == 0)
    def _():
        qb_ref[...] = (q_ref[0] * (hd ** -0.5)).astype(_BF16)
        m_ref[...] = jnp.zeros(m_ref.shape, _F32)
        l_ref[...] = jnp.zeros(l_ref.shape, _F32)

    flat = lambda ref: ref[0, 0].reshape(page * n_h, hd)
    k = jnp.concatenate([flat(k0_ref), flat(k1_ref)], axis=0)
    v = jnp.concatenate([flat(v0_ref), flat(v1_ref)], axis=0)
    ksum_ref[j] = jnp.sum(k.reshape(blk, n_h, hd), axis=0)

    s = lax.dot_general(qb_ref[...], k.astype(_BF16), _NT, preferred_element_type=_F32) + bias_ref[...]
    m_loc = jnp.max(s, axis=1, keepdims=True)
    p = jnp.exp(s - m_loc)
    l_loc = jnp.sum(p, axis=1, keepdims=True)
    acc_ref[j] = jnp.dot(p.astype(_BF16), v.astype(_BF16), preferred_element_type=_F32)
    m_blk = m_loc - slope * (past_len + tpos - (j * blk).astype(_F32))
    m_ref[...] = jnp.where(blk_iota == j, m_blk, m_ref[...])
    l_ref[...] = jnp.where(blk_iota == j, l_loc, l_ref[...])

    @pl.when(j == nblk - 1)
    def _():
        qf = q_ref[0]

        def gate_col(jj, gate):
            kmean = ksum_ref[jj] * (1.0 / blk)
            kmean_rows = jnp.concatenate([kmean] * n_t, axis=0)
            col = jnp.sum(qf * kmean_rows, axis=1, keepdims=True)
            return jnp.where(blk_iota == jj, col, gate)

        gate = lax.fori_loop(0, nb, gate_col, jnp.zeros((rows, nb), _F32))
        sel = _topk_select(gate, blk_iota, nb, axis=1)

        s_own = lax.dot_general(qb_ref[...], kn_ref[0].astype(_BF16), _NT,
                                preferred_element_type=_F32) + bown_ref[...]
        m_all = m_ref[...]
        m_fin = jnp.maximum(jnp.max(s_own, axis=1, keepdims=True),
                            jnp.max(jnp.where(sel, m_all, -jnp.inf), axis=1, keepdims=True))
        w = jnp.where(sel, jnp.exp(m_all - m_fin), 0.0)
        p_own = jnp.exp(s_own - m_fin)
        denom = jnp.sum(w * l_ref[...], axis=1, keepdims=True) + jnp.sum(p_own, axis=1, keepdims=True)
        out = jnp.dot(p_own.astype(_BF16), vn_ref[0].astype(_BF16), preferred_element_type=_F32)

        def merge(jj, out):
            w_j = jnp.sum(jnp.where(blk_iota == jj, w, 0.0), axis=1, keepdims=True)
            return out + w_j * acc_ref[jj]

        o_ref[0] = lax.fori_loop(0, nb, merge, out) / denom


def _sample_bias_tables(slopes, n_t):
    n_h = slopes.shape[0]
    row_h = jnp.tile(jnp.arange(n_h), n_t)[:, None]
    row_t = jnp.repeat(jnp.arange(n_t), n_h)[:, None]
    col_key = jnp.repeat(jnp.arange(_MOBA_BLOCK), n_h)[None, :]
    col_h = jnp.tile(jnp.arange(n_h), _MOBA_BLOCK)[None, :]
    bias = jnp.where(row_h == col_h, slopes[col_h] * col_key.astype(_F32), _NEG)
    own_r = jnp.repeat(jnp.arange(n_t), n_h)[None, :]
    own_h = jnp.tile(jnp.arange(n_h), n_t)[None, :]
    ok = jnp.logical_and(row_h == own_h, own_r <= row_t)
    bown = jnp.where(ok, -slopes[own_h] * (row_t - own_r).astype(_F32), _NEG)
    return bias.astype(_F32), bown.astype(_F32), slopes[row_h[:, 0]][:, None], row_t.astype(_F32)


def _moba_sample(q, kn, vn, cache_k, cache_v, page_table, tables, *, li):
    n_seq, rows, hd = q.shape
    _, _, page_size, n_h, _ = cache_k.shape
    n_pages = page_table.shape[1]
    assert _MOBA_BLOCK == 2 * page_size
    nb = n_pages // 2
    kern = functools.partial(_moba_sample_kernel, past_len=float(n_pages * page_size))
    tok = pl.BlockSpec((1, rows, hd), lambda n, j, pt: (n, 0, 0))
    page = lambda half: pl.BlockSpec((1, 1, page_size, n_h, hd),
                                     lambda n, j, pt: (pt[n, 2 * j + half], li, 0, 0, 0))
    const = lambda arr: pl.BlockSpec(arr.shape, lambda n, j, pt: (0, 0))
    grid_spec = pltpu.PrefetchScalarGridSpec(
        num_scalar_prefetch=1,
        grid=(n_seq, nb),
        in_specs=[tok, tok, tok, page(0), page(1), page(0), page(1)] + [const(t) for t in tables],
        out_specs=tok,
        scratch_shapes=[
            pltpu.VMEM((rows, hd), _BF16),
            pltpu.VMEM((nb, n_h, hd), _F32),
            pltpu.VMEM((rows, nb), _F32),
            pltpu.VMEM((rows, nb), _F32),
            pltpu.VMEM((nb, rows, hd), _F32),
        ],
    )
    return pl.pallas_call(
        kern,
        grid_spec=grid_spec,
        out_shape=jax.ShapeDtypeStruct((n_seq, rows, hd), _F32),
        compiler_params=_cparams("parallel", "arbitrary"),
        name="moba_sample",
    )(page_table, q, kn, vn, cache_k, cache_k, cache_v, cache_v, *tables)


def _lru_gates(xc, wr_ref, br, wi_ref, bi, lam):
    bw = wr_ref.shape[1]
    xb = xc.astype(wr_ref.dtype)
    r_parts, i_parts = [], []
    for g in range(wr_ref.shape[0]):
        xg = xb[:, g * bw:(g + 1) * bw]
        r_parts.append(_mxu(xg, wr_ref[g]))
        i_parts.append(_mxu(xg, wi_ref[g]))
    r = jax.nn.sigmoid(jnp.concatenate(r_parts, axis=1) + br)
    i = jax.nn.sigmoid(jnp.concatenate(i_parts, axis=1) + bi)
    softplus_neg_lam = jnp.maximum(-lam, 0.0) + jnp.log1p(jnp.exp(-jnp.abs(lam)))
    log_a = (-_LRU_C * softplus_neg_lam) * r
    a = jnp.exp(log_a)
    bx = jnp.sqrt(1.0 - a * a) * (i * xc)
    return a, bx


def _scan_rows(a, b):
    n = a.shape[0]
    row = lax.broadcasted_iota(jnp.int32, a.shape, 0)
    s = 1
    while s < n:
        keep = row >= s
        a_sh = jnp.where(keep, pltpu.roll(a, s, 0), 1.0)
        b_sh = jnp.where(keep, pltpu.roll(b, s, 0), 0.0)
        b = a * b_sh + b
        a = a * a_sh
        s *= 2
    return a, b


def _out_proj(mix_a, mix_b, wout_ref, x):
    half = wout_ref.shape[0] // 2
    return x + _mxu(mix_a, wout_ref[0:half, :]) + _mxu(mix_b, wout_ref[half:2 * half, :])


def _even_out_prompt_kernel(xa_ref, ga_ref, gb_ref, o_ref, x_ref, cw_ref, cb_ref, wr_ref, br_ref, wi_ref,
                            bi_ref, lam_ref, wout_ref, y_ref, hlast_ref, ext_ref, h_ref):
    tile = xa_ref.shape[1]
    pre = _SUBLANES
    taps = cw_ref.shape[0]

    @pl.when(pl.program_id(1) == 0)
    def _():
        ext_ref[0:pre, :] = jnp.zeros((pre, ext_ref.shape[1]), _F32)
        h_ref[...] = jnp.zeros(h_ref.shape, _F32)

    xa = xa_ref[0]
    ext_ref[pre:pre + tile, :] = xa
    xc = cb_ref[...] + cw_ref[taps - 1:taps, :] * xa
    for k in range(taps - 1):
        off = pre - (taps - 1) + k
        xc = xc + cw_ref[k:k + 1, :] * ext_ref[off:off + tile, :]
    ext_ref[0:pre, :] = xa[tile - pre:tile, :]

    a, bx = _lru_gates(xc, wr_ref, br_ref[...], wi_ref, bi_ref[...], lam_ref[...])
    a_cum, h = _scan_rows(a, bx)
    h = h + a_cum * h_ref[...]
    h_ref[...] = h[tile - 1:tile, :]
    hlast_ref[0] = h[tile - 1:tile, :]

    y_ref[0] = _out_proj(h * _silu(ga_ref[0]), o_ref[0] * _silu(gb_ref[0]), wout_ref, x_ref[0])


def _even_out_prompt(z3, o, x, cw, cb, wr, br, wi, bi, lam, wout, *, tile, w_a):
    b, s, d = x.shape
    nc = w_a // d
    assert nc == 1
    row = lambda col: pl.BlockSpec((1, tile, d), lambda bi_, t: (bi_, t, col))
    full = lambda arr: pl.BlockSpec(arr.shape, lambda bi_, t: (0,) * arr.ndim)
    vec = pl.BlockSpec((1, d), lambda bi_, t: (0, 0))
    return pl.pallas_call(
        _even_out_prompt_kernel,
        grid=(b, s // tile),
        in_specs=[row(0), row(1), row(5), row(0), row(0), full(cw), vec, full(wr), vec, full(wi), vec, vec,
                  full(wout)],
        out_specs=[row(0), pl.BlockSpec((1, 1, d), lambda bi_, t: (bi_, 0, 0))],
        out_shape=[jax.ShapeDtypeStruct((b, s, d), _F32), jax.ShapeDtypeStruct((b, 1, d), _F32)],
        scratch_shapes=[pltpu.VMEM((_SUBLANES + tile, d), _F32), pltpu.VMEM((1, d), _F32)],
        compiler_params=_cparams("parallel", "arbitrary"),
        name="even_out_prompt",
    )(z3, z3, z3, o, x, cw, cb.reshape(1, d), wr, br.reshape(1, d), wi, bi.reshape(1, d), lam.reshape(1, d), wout)


def _even_out_sample_kernel(z_ref, o_ref, x_ref, pre_ref, h0_ref, cw_ref, cb_ref, wr_ref, br_ref, wi_ref,
                            bi_ref, lam_ref, wout_ref, y_ref, hlast_ref, cstate_ref, *, n_seq):
    d = x_ref.shape[1]
    rows = x_ref.shape[0]
    taps = cw_ref.shape[0]
    xa = z_ref[:, 0:d]
    ext = jnp.concatenate([pre_ref[...], xa], axis=0)
    xc = cb_ref[...]
    for k in range(taps):
        xc = xc + cw_ref[k:k + 1, :] * ext[k * n_seq:k * n_seq + rows, :]
    cstate_ref[...] = ext[rows:rows + (taps - 1) * n_seq, :]

    a, bx = _lru_gates(xc, wr_ref, br_ref[...], wi_ref, bi_ref[...], lam_ref[...])
    h = h0_ref[...]
    hs = []
    for t in range(rows // n_seq):
        h = a[t * n_seq:(t + 1) * n_seq, :] * h + bx[t * n_seq:(t + 1) * n_seq, :]
        hs.append(h)
    hlast_ref[...] = h
    ha = jnp.concatenate(hs, axis=0)
    y_ref[...] = _out_proj(ha * _silu(z_ref[:, d:2 * d]), o_ref[...] * _silu(z_ref[:, 5 * d:6 * d]), wout_ref,
                           x_ref[...])


def _even_out_sample(z, o, x, pre, h0, cw, cb, wr, br, wi, bi, lam, wout, *, n_seq):
    rows, d = x.shape
    args = (z, o, x, pre, h0, cw, cb.reshape(1, d), wr, br.reshape(1, d), wi, bi.reshape(1, d), lam.reshape(1, d),
            wout)
    full = lambda arr: pl.BlockSpec(arr.shape, lambda i: (0,) * arr.ndim)
    once = lambda arr: pl.BlockSpec(arr.shape, lambda i: (0,) * arr.ndim, pipeline_mode=pl.Buffered(1))
    outs = [jax.ShapeDtypeStruct((rows, d), _F32), jax.ShapeDtypeStruct((n_seq, d), _F32),
            jax.ShapeDtypeStruct(pre.shape, _F32)]
    return pl.pallas_call(
        functools.partial(_even_out_sample_kernel, n_seq=n_seq),
        grid=(1,),
        in_specs=[once(a) for a in args],
        out_specs=[full(o_) for o_ in outs],
        out_shape=outs,
        compiler_params=_cparams("arbitrary"),
        name="even_out_sample",
    )(*args)


def _pool_project(pooled, pw_ref, pb, ps):
    gw = pw_ref.shape[1]
    pooled = pooled.astype(pw_ref.dtype)
    parts = [_mxu(pooled[:, g * gw:(g + 1) * gw], pw_ref[g]) for g in range(pw_ref.shape[0])]
    return (jnp.concatenate(parts, axis=1) + pb) * ps


def _layernorm(x, g, b):
    xc = x - jnp.mean(x, axis=-1, keepdims=True)
    var = jnp.mean(xc * xc, axis=-1, keepdims=True)
    return xc * lax.rsqrt(var + _EPS) * g + b


def _odd_out_prompt_kernel(xc_ref, gc_ref, da_ref, db_ref, gd_ref, x_ref, pw_ref, pb_ref, ps_ref, dw_ref,
                           dbias_ref, lg_ref, lb_ref, wout_ref, nf_ref, y_ref, dstate_ref, pext_ref, uext_ref,
                           *, final_norm):
    tile = xc_ref.shape[1]
    d = xc_ref.shape[2]
    ppre = pext_ref.shape[0] - tile
    upre = uext_ref.shape[0] - tile
    taps = dw_ref.shape[0]
    t_idx = pl.program_id(1)

    @pl.when(t_idx == 0)
    def _():
        pext_ref[0:ppre, :] = jnp.zeros((ppre, d), _F32)
        uext_ref[0:upre, :] = jnp.zeros((upre, d), _F32)

    xc = xc_ref[0]
    pext_ref[ppre:ppre + tile, :] = xc
    pos = t_idx * tile + lax.broadcasted_iota(jnp.int32, (tile, 1), 0)
    gw = d // len(_POOL_WINDOWS)
    parts = []
    for g, win in enumerate(_POOL_WINDOWS):
        ssum = xc[:, g * gw:(g + 1) * gw]
        for sft in range(1, win):
            ssum = ssum + pext_ref[ppre - sft:ppre - sft + tile, g * gw:(g + 1) * gw]
        cnt = jnp.minimum(win, pos + 1).astype(_F32)
        parts.append(ssum / cnt)
    pooled = jnp.concatenate(parts, axis=1) - xc
    pext_ref[0:ppre, :] = xc[tile - ppre:tile, :]
    yc = _pool_project(pooled, pw_ref, pb_ref[...], ps_ref[...])

    u = da_ref[0] * jax.nn.sigmoid(db_ref[0])
    uext_ref[upre:upre + tile, :] = u
    dc = dbias_ref[...] + dw_ref[taps - 1:taps, :] * u
    for k in range(taps - 1):
        off = upre - (taps - 1) + k
        dc = dc + dw_ref[k:k + 1, :] * uext_ref[off:off + tile, :]
    uext_ref[0:upre, :] = u[tile - upre:tile, :]
    dstate_ref[0] = u[tile - upre:tile, :]
    yd = _silu(_layernorm(dc, lg_ref[...], lb_ref[...]))

    y = _out_proj(yc * _silu(gc_ref[0]), yd * _silu(gd_ref[0]), wout_ref, x_ref[0])
    if final_norm:
        y = _rms(y, nf_ref[...])
    y_ref[0] = y


def _odd_out_prompt(z3, x, pw, pb, ps, dw, dbias, lg, lb, wout, nf, *, tile, final_norm):
    b, s, d = x.shape
    ppre = 16
    upre = 32
    assert max(_POOL_WINDOWS) - 1 <= ppre and dw.shape[0] - 1 <= upre
    row = lambda col: pl.BlockSpec((1, tile, d), lambda bi_, t: (bi_, t, col))
    full = lambda arr: pl.BlockSpec(arr.shape, lambda bi_, t: (0,) * arr.ndim)
    vec = pl.BlockSpec((1, d), lambda bi_, t: (0, 0))
    return pl.pallas_call(
        functools.partial(_odd_out_prompt_kernel, final_norm=final_norm),
        grid=(b, s // tile),
        in_specs=[row(0), row(1), row(2), row(3), row(4), row(0), full(pw), vec, vec, full(dw), vec, vec, vec,
                  full(wout), vec],
        out_specs=[row(0), pl.BlockSpec((1, upre, d), lambda bi_, t: (bi_, 0, 0))],
        out_shape=[jax.ShapeDtypeStruct((b, s, d), _F32), jax.ShapeDtypeStruct((b, upre, d), _F32)],
        scratch_shapes=[pltpu.VMEM((ppre + tile, d), _F32), pltpu.VMEM((upre + tile, d), _F32)],
        compiler_params=_cparams("parallel", "arbitrary"),
        name="odd_out_prompt",
    )(z3, z3, z3, z3, z3, x, pw, pb.reshape(1, d), ps.reshape(1, d), dw, dbias.reshape(1, d), lg.reshape(1, d),
      lb.reshape(1, d), wout, nf.reshape(1, d))


def _odd_out_sample_kernel(z_ref, x_ref, ppre_ref, upre_ref, pw_ref, pb_ref, ps_ref, dw_ref, dbias_ref, lg_ref,
                           lb_ref, wout_ref, nf_ref, y_ref, pstate_ref, dstate_ref, *, n_seq, pos0, final_norm):
    rows, d = x_ref.shape
    taps = dw_ref.shape[0]
    pbuf = ppre_ref.shape[0] // n_seq
    xc = z_ref[:, 0:d]
    pext = jnp.concatenate([ppre_ref[...], xc], axis=0)
    pstate_ref[...] = pext[rows:rows + pbuf * n_seq, :]
    t_of_row = lax.broadcasted_iota(jnp.int32, (rows, 1), 0) // n_seq
    gw = d // len(_POOL_WINDOWS)
    parts = []
    for g, win in enumerate(_POOL_WINDOWS):
        ssum = xc[:, g * gw:(g + 1) * gw]
        for sft in range(1, win):
            lo = (pbuf - sft) * n_seq
            ssum = ssum + pext[lo:lo + rows, g * gw:(g + 1) * gw]
        cnt = jnp.minimum(win, pos0 + t_of_row + 1).astype(_F32)
        parts.append(ssum / cnt)
    pooled = jnp.concatenate(parts, axis=1) - xc
    yc = _pool_project(pooled, pw_ref, pb_ref[...], ps_ref[...])

    u = z_ref[:, 2 * d:3 * d] * jax.nn.sigmoid(z_ref[:, 3 * d:4 * d])
    uext = jnp.concatenate([upre_ref[...], u], axis=0)
    dstate_ref[...] = uext[rows:rows + (taps - 1) * n_seq, :]
    dc = dbias_ref[...]
    for k in range(taps):
        dc = dc + dw_ref[k:k + 1, :] * uext[k * n_seq:k * n_seq + rows, :]
    yd = _silu(_layernorm(dc, lg_ref[...], lb_ref[...]))

    y = _out_proj(yc * _silu(z_ref[:, d:2 * d]), yd * _silu(z_ref[:, 4 * d:5 * d]), wout_ref, x_ref[...])
    if final_norm:
        y = _rms(y, nf_ref[...])
    y_ref[...] = y


def _odd_out_sample(z, x, ppre, upre, pw, pb, ps, dw, dbias, lg, lb, wout, nf, *, n_seq, pos0, final_norm):
    rows, d = x.shape
    args = (z, x, ppre, upre, pw, pb.reshape(1, d), ps.reshape(1, d), dw, dbias.reshape(1, d), lg.reshape(1, d),
            lb.reshape(1, d), wout, nf.reshape(1, d))
    full = lambda arr: pl.BlockSpec(arr.shape, lambda i: (0,) * arr.ndim)
    once = lambda arr: pl.BlockSpec(arr.shape, lambda i: (0,) * arr.ndim, pipeline_mode=pl.Buffered(1))
    outs = [jax.ShapeDtypeStruct((rows, d), _F32), jax.ShapeDtypeStruct(ppre.shape, _F32),
            jax.ShapeDtypeStruct(upre.shape, _F32)]
    return pl.pallas_call(
        functools.partial(_odd_out_sample_kernel, n_seq=n_seq, pos0=pos0, final_norm=final_norm),
        grid=(1,),
        in_specs=[once(a) for a in args],
        out_specs=[full(o_) for o_ in outs],
        out_shape=outs,
        compiler_params=_cparams("arbitrary"),
        name="odd_out_sample",
    )(*args)


def _time_major(state):
    n, k, c = state.shape
    return jnp.transpose(state, (1, 0, 2)).reshape(k * n, c)


def _batch_major(state_tm, n):
    kn, c = state_tm.shape
    return jnp.transpose(state_tm.reshape(kn // n, n, c), (1, 0, 2))


def kernel(x_prompt, x_sample, cache_k, cache_v, page_table, state_lru_h, state_lru_conv, state_pool, state_dconv, norm_even, w_in_even, lru_conv_w, lru_conv_b, lru_wr, lru_br, lru_wi, lru_bi, lru_lambda, w_out_even, norm_odd, w_in_odd, pool_w, pool_b, pool_scale, dconv_w, dconv_b, dln_g, dln_b, w_out_odd, norm_final):
    bp, seq, d = x_prompt.shape
    n_seq, n_t, _ = x_sample.shape
    n_even, n_odd = w_in_even.shape[0], w_in_odd.shape[0]
    depth = n_even + n_odd
    n_phys, _, page_size, n_heads, hd = cache_k.shape
    w_b = n_heads * hd
    n_pages = page_table.shape[1]
    past_len = n_pages * page_size
    lru_taps = lru_conv_w.shape[1]
    pool_buf = state_pool.shape[2]
    dconv_taps = dconv_w.shape[1]
    assert w_b == d and seq % _MOBA_BLOCK == 0 and past_len % _MOBA_BLOCK == 0 and n_t <= _SUBLANES
    assert seq // _MOBA_BLOCK >= _MOBA_TOPK and past_len // _MOBA_BLOCK >= _MOBA_TOPK

    slopes = 2.0 ** (-8.0 * (jnp.arange(n_heads, dtype=_F32) + 1.0) / n_heads)
    tables = _sample_bias_tables(slopes, n_t)
    to_bf16 = lambda w: w.astype(_BF16)
    keep = lambda w: w

    def head_rows(zs, col):
        part = zs[:, col * d:(col + 1) * d].reshape(n_t, n_seq, n_heads, hd)
        return jnp.transpose(part, (1, 0, 2, 3)).reshape(n_seq, n_t * n_heads, hd)

    xp = x_prompt
    xs = jnp.transpose(x_sample, (1, 0, 2)).reshape(n_t * n_seq, d)
    tile_p = 256
    tm_p = 1024

    kp_l, vp_l, ks_l, vs_l = [], [], [], []
    hp_l, hs_l, cp_l, cs_l = [], [], [], []
    pp_l, ps_l, dp_l, ds_l = [], [], [], []
    for layer in range(depth):
        li = layer // 2
        last = layer == depth - 1
        if layer % 2 == 0:
            lru_w = lambda cast: (lru_conv_w[li], lru_conv_b[li], cast(lru_wr[li]), lru_br[li], cast(lru_wi[li]),
                                  lru_bi[li], lru_lambda[li], cast(w_out_even[li]))
            z = _in_proj(xp.reshape(bp * seq, d), norm_even[li], to_bf16(w_in_even[li]), tm=tm_p, tn=512)
            z3 = z.reshape(bp, seq, z.shape[1])
            o = _moba_prompt(z3, slopes, q_col=2, k_col=3, v_col=4, w_b=w_b, hd=hd)
            xp, h_last = _even_out_prompt(z3, o, xp, *lru_w(to_bf16), tile=tile_p, w_a=d)
            kp_l.append(z3[:, :, 3 * d:4 * d].reshape(bp, seq, n_heads, hd))
            vp_l.append(z3[:, :, 4 * d:5 * d].reshape(bp, seq, n_heads, hd))
            hp_l.append(h_last.reshape(bp, d))
            cp_l.append(z3[:, seq - (lru_taps - 1):, 0:d])
            zs = _in_proj(xs, norm_even[li], w_in_even[li], tm=n_t * n_seq, tn=512)
            kn, vn = head_rows(zs, 3), head_rows(zs, 4)
            os_ = _moba_sample(head_rows(zs, 2), kn, vn, cache_k, cache_v, page_table, tables, li=li)
            os_tm = jnp.transpose(os_.reshape(n_seq, n_t, w_b), (1, 0, 2)).reshape(n_t * n_seq, w_b)
            xs, hs_last, cs_tm = _even_out_sample(zs, os_tm, xs, _time_major(state_lru_conv[:, li]),
                                                  state_lru_h[:, li], *lru_w(keep), n_seq=n_seq)
            ks_l.append(kn.reshape(n_seq, n_t, n_heads, hd))
            vs_l.append(vn.reshape(n_seq, n_t, n_heads, hd))
            hs_l.append(hs_last)
            cs_l.append(_batch_major(cs_tm, n_seq))
        else:
            odd_w = lambda cast: (cast(pool_w[li]), pool_b[li], pool_scale[li], dconv_w[li], dconv_b[li], dln_g[li],
                                  dln_b[li], cast(w_out_odd[li]), norm_final)
            z = _in_proj(xp.reshape(bp * seq, d), norm_odd[li], to_bf16(w_in_odd[li]), tm=tm_p, tn=512)
            z3 = z.reshape(bp, seq, z.shape[1])
            xp, dstate = _odd_out_prompt(z3, xp, *odd_w(to_bf16), tile=tile_p, final_norm=last)
            pp_l.append(z3[:, seq - pool_buf:, 0:d])
            dp_l.append(dstate[:, dstate.shape[1] - (dconv_taps - 1):, :])
            zs = _in_proj(xs, norm_odd[li], w_in_odd[li], tm=n_t * n_seq, tn=512)
            xs, ps_tm, ds_tm = _odd_out_sample(zs, xs, _time_major(state_pool[:, li]),
                                               _time_major(state_dconv[:, li]), *odd_w(keep), n_seq=n_seq,
                                               pos0=past_len, final_norm=last)
            ps_l.append(_batch_major(ps_tm, n_seq))
            ds_l.append(_batch_major(ds_tm, n_seq))

    if depth % 2 == 1:
        raise NotImplementedError("final norm is fused into the last odd layer")
    y_prompt = xp
    y_sample = jnp.transpose(xs.reshape(n_t, n_seq, d), (1, 0, 2))
    stack = lambda xs_: jnp.stack(xs_, axis=1)
    return (y_prompt, y_sample, stack(kp_l), stack(vp_l), stack(ks_l), stack(vs_l),
            stack(hp_l), stack(hs_l), stack(cp_l), stack(cs_l),
            stack(pp_l), stack(ps_l), stack(dp_l), stack(ds_l))
```

```python
import functools

import jax
import jax.numpy as jnp
from jax import lax
from jax.experimental import pallas as pl
from jax.experimental.pallas import tpu as pltpu

_MOBA_BLOCK = 256
_MOBA_TOPK = 3
_MOBA_HEAD_GROUP = 4
_MOBA_BLOCKS_PER_TRIP = 4
_N_HEADS = 16
_LRU_BLOCKS = 8
_LRU_C = 8.0
_POOL_WINDOWS = (2, 4, 8, 16)
_EPS = 1e-6
_NEG = -1e30

_LANES = 128
_SUBLANES = 8
_BF16_ROWS = 16
_LOG2E = 1.4426950408889634
_VMEM_LIMIT_BYTES = 56 * 1024 * 1024

_BF16 = jnp.bfloat16
_F32 = jnp.float32
_NT = (((1,), (1,)), ((), ()))


def _cparams(*sem):
    return pltpu.CompilerParams(dimension_semantics=sem, vmem_limit_bytes=_VMEM_LIMIT_BYTES)


def _mxu(x, w):
    if w.dtype == _BF16:
        return jnp.dot(x.astype(_BF16), w, preferred_element_type=_F32)
    return jnp.dot(x, w, precision=lax.Precision.HIGHEST, preferred_element_type=_F32)


def _silu(x):
    return x * jax.nn.sigmoid(x)


def _rms(x, g):
    return x * lax.rsqrt(jnp.mean(x * x, axis=-1, keepdims=True) + _EPS) * g


def _in_proj_kernel(x_ref, g_ref, w_ref, z_ref, h_ref):
    @pl.when(pl.program_id(1) == 0)
    def _():
        h_ref[...] = _rms(x_ref[...], g_ref[...]).astype(h_ref.dtype)

    z_ref[...] = _mxu(h_ref[...], w_ref[...])


def _in_proj(x, g, w, *, tm, tn):
    m, d = x.shape
    n = w.shape[1]
    assert m % tm == 0 and n % tn == 0
    return pl.pallas_call(
        _in_proj_kernel,
        grid=(m // tm, n // tn),
        in_specs=[
            pl.BlockSpec((tm, d), lambda i, j: (i, 0)),
            pl.BlockSpec((1, d), lambda i, j: (0, 0)),
            pl.BlockSpec((d, tn), lambda i, j: (0, j)),
        ],
        out_specs=pl.BlockSpec((tm, tn), lambda i, j: (i, j)),
        out_shape=jax.ShapeDtypeStruct((m, n), _F32),
        scratch_shapes=[pltpu.VMEM((tm, d), w.dtype)],
        compiler_params=_cparams("parallel", "arbitrary"),
        name="in_proj",
    )(x, g.reshape(1, d), w)


def _topk_select(gate, blk_iota, nblk, axis):
    sel = jnp.zeros(gate.shape, jnp.bool_)
    g = gate
    for _ in range(_MOBA_TOPK):
        mx = jnp.max(g, axis=axis, keepdims=True)
        first = jnp.min(jnp.where(g == mx, blk_iota, nblk), axis=axis, keepdims=True)
        pick = blk_iota == first
        sel = jnp.logical_or(sel, pick)
        g = jnp.where(pick, -jnp.inf, g)
    return sel


def _moba_prompt_kernel(slopes_ref, q_ref, k_ref, v_ref, o_ref,
                        kb_ref, vt_ref, km_ref, kbias_ref, qb_ref, pen_ref, acc_ref, *, hd):
    blk = _MOBA_BLOCK
    hp = pl.program_id(1)
    qi = pl.program_id(2)
    nblk = km_ref.shape[0]
    width = q_ref.shape[2]
    n_h = width // hd

    va = vt_ref.shape[2]

    @pl.when(qi == 0)
    def _():
        ones_rows = (lax.broadcasted_iota(jnp.int32, (va - hd, blk), 0) == 0).astype(_F32)
        for j in range(nblk):
            kj = k_ref[0, j * blk:(j + 1) * blk, :]
            kb_ref[j * blk:(j + 1) * blk, :] = kj.astype(_BF16)
            km_ref[j:j + 1, :] = jnp.sum(kj, axis=0, keepdims=True) * (1.0 / blk)
            vt = v_ref[0, j * blk:(j + 1) * blk, :].T
            for hh in range(n_h):
                vt_ref[j, hh] = jnp.concatenate([vt[hh * hd:(hh + 1) * hd, :], ones_rows], axis=0).astype(_BF16)
        key_r = lax.broadcasted_iota(jnp.int32, (blk, blk), 0).astype(_F32)
        for hh in range(n_h):
            kbias_ref[hh] = (slopes_ref[hp * n_h + hh] * _LOG2E) * key_r

    qt = q_ref[0].T
    d_iota = lax.broadcasted_iota(jnp.int32, (width, blk), 0)
    blk_iota = lax.broadcasted_iota(jnp.int32, (nblk, blk), 0)
    key_i = lax.broadcasted_iota(jnp.int32, (blk, blk), 0)
    qry_i = lax.broadcasted_iota(jnp.int32, (blk, blk), 1)
    past = blk_iota < qi
    q_start = pl.multiple_of(qi * blk, blk)
    slopes = [slopes_ref[hp * n_h + hh] * _LOG2E for hh in range(n_h)]

    m_init = []
    for hh in range(n_h):
        qm = jnp.where(jnp.logical_and(d_iota >= hh * hd, d_iota < (hh + 1) * hd), qt, 0.0)
        gate = jnp.dot(km_ref[...], qm, precision=lax.Precision.HIGHEST, preferred_element_type=_F32)
        sel = _topk_select(jnp.where(past, gate, _NEG), blk_iota, nblk, axis=0)
        pen_ref[hh] = jnp.where(jnp.logical_and(sel, past), 0.0, _NEG)
        qb = (qm * (hd ** -0.5 * _LOG2E)).astype(_BF16)
        qb_ref[hh] = qb
        s = jnp.dot(kb_ref[pl.ds(q_start, blk), :], qb, preferred_element_type=_F32) + kbias_ref[hh]
        s = jnp.where(key_i <= qry_i, s, _NEG)
        m0 = jnp.max(s, axis=0, keepdims=True)
        m_init.append(m0)
        acc_ref[hh] = jnp.dot(vt_ref[qi, hh], jnp.exp2(s - m0).astype(_BF16), preferred_element_type=_F32)

    grp = _MOBA_BLOCKS_PER_TRIP

    def body(jj, ms):
        j0 = grp * jj
        kgrp = kb_ref[pl.ds(pl.multiple_of(j0 * blk, grp * blk), grp * blk), :]
        sg = [jnp.dot(kgrp, qb_ref[hh], preferred_element_type=_F32) for hh in range(n_h)]
        new_m = []
        for hh in range(n_h):
            pen = pen_ref[hh]
            ts, offs = [], []
            for u in range(grp):
                pen_u = jnp.sum(jnp.where(blk_iota == j0 + u, pen, 0.0), axis=0, keepdims=True)
                offs.append(pen_u - slopes[hh] * ((qi - j0 - u) * blk).astype(_F32))
                ts.append(sg[hh][u * blk:(u + 1) * blk, :] + kbias_ref[hh])
            m_new = ms[hh]
            for u in range(grp):
                m_new = jnp.maximum(m_new, jnp.max(ts[u], axis=0, keepdims=True) + offs[u])
            alpha = jnp.exp2(ms[hh] - m_new)
            pv = None
            for u in range(grp):
                pu = jnp.dot(vt_ref[j0 + u, hh], jnp.exp2(ts[u] - (m_new - offs[u])).astype(_BF16),
                             preferred_element_type=_F32)
                pv = pu if pv is None else pv + pu
            acc_ref[hh] = alpha * acc_ref[hh] + pv
            new_m.append(m_new)
        return tuple(new_m)

    lax.fori_loop(0, (qi + grp - 1) // grp, body, tuple(m_init))
    outs = []
    for hh in range(n_h):
        acc = acc_ref[hh]
        outs.append(acc[0:hd, :] / acc[hd:hd + 1, :])
    o_ref[0] = jnp.concatenate(outs, axis=0).T


def _moba_prompt(z3, slopes, *, q_col, k_col, v_col, w_b, hd):
    b, s, _ = z3.shape
    width = _MOBA_HEAD_GROUP * hd
    n_h = width // hd
    groups = w_b // width
    nblk = s // _MOBA_BLOCK
    assert nblk % _MOBA_BLOCKS_PER_TRIP == 0 and w_b % width == 0
    kern = functools.partial(_moba_prompt_kernel, hd=hd)
    resident = lambda col: pl.BlockSpec((1, s, width), lambda bi, hp, qi, sl: (bi, 0, col * groups + hp),
                                        pipeline_mode=pl.Buffered(1))
    grid_spec = pltpu.PrefetchScalarGridSpec(
        num_scalar_prefetch=1,
        grid=(b, groups, nblk),
        in_specs=[
            pl.BlockSpec((1, _MOBA_BLOCK, width), lambda bi, hp, qi, sl: (bi, qi, q_col * groups + hp)),
            resident(k_col),
            resident(v_col),
        ],
        out_specs=pl.BlockSpec((1, _MOBA_BLOCK, width), lambda bi, hp, qi, sl: (bi, qi, hp)),
        scratch_shapes=[
            pltpu.VMEM((s, width), _BF16),
            pltpu.VMEM((nblk, n_h, hd + _BF16_ROWS, _MOBA_BLOCK), _BF16),
            pltpu.VMEM((nblk, width), _F32),
            pltpu.VMEM((n_h, _MOBA_BLOCK, _MOBA_BLOCK), _F32),
            pltpu.VMEM((n_h, width, _MOBA_BLOCK), _BF16),
            pltpu.VMEM((n_h, nblk, _MOBA_BLOCK), _F32),
            pltpu.VMEM((n_h, hd + _BF16_ROWS, _MOBA_BLOCK), _F32),
        ],
    )
    return pl.pallas_call(
        kern,
        grid_spec=grid_spec,
        out_shape=jax.ShapeDtypeStruct((b, s, w_b), _F32),
        compiler_params=_cparams("parallel", "parallel", "arbitrary"),
        name="moba_prompt",
    )(slopes, z3, z3, z3)


def _moba_sample_kernel(pt_ref, q_ref, kn_ref, vn_ref, k0_ref, k1_ref, v0_ref, v1_ref,
                        kbias_ref, bown_ref, slope_ref, tpos_ref, o_ref,
                        qb_ref, ks_ref, m_ref, l_ref, acc_ref):
    blk = _MOBA_BLOCK
    j = pl.program_id(1)
    nblk = pl.num_programs(1)
    _, n_h, tp, hd = q_ref.shape
    nb = m_ref.shape[2]
    blk_iota = lax.broadcasted_iota(jnp.int32, (n_h, tp, nb), 2)
    ks_iota = lax.broadcasted_iota(jnp.int32, (n_h, hd, nb), 2)
    qk = (((2,), (1,)), ((0,), (0,)))
    pv = (((2,), (2,)), ((0,), (0,)))

    @pl.when(j == 0)
    def _():
        qb_ref[...] = (q_ref[0] * (hd ** -0.5)).astype(_BF16)
        m_ref[...] = jnp.zeros(m_ref.shape, _F32)
        l_ref[...] = jnp.zeros(l_ref.shape, _F32)
        ks_ref[...] = jnp.zeros(ks_ref.shape, _F32)

    k = jnp.concatenate([k0_ref[0, 0], k1_ref[0, 0]], axis=2)
    v = jnp.concatenate([v0_ref[0, 0], v1_ref[0, 0]], axis=2)
    ks_ref[...] = jnp.where(ks_iota == j, jnp.sum(k, axis=2, keepdims=True), ks_ref[...])
    s = lax.dot_general(qb_ref[...], k.astype(_BF16), qk, preferred_element_type=_F32) + kbias_ref[...]
    m_loc = jnp.max(s, axis=2, keepdims=True)
    p = jnp.exp(s - m_loc)
    l_loc = jnp.sum(p, axis=2, keepdims=True)
    acc_ref[j] = lax.dot_general(p.astype(_BF16), v.astype(_BF16), pv, preferred_element_type=_F32)
    m_blk = m_loc - slope_ref[...] * (tpos_ref[...] - (j * blk).astype(_F32))
    m_ref[...] = jnp.where(blk_iota == j, m_blk, m_ref[...])
    l_ref[...] = jnp.where(blk_iota == j, l_loc, l_ref[...])

    @pl.when(j == nblk - 1)
    def _():
        gate = lax.dot_general(q_ref[0], ks_ref[...] * (1.0 / blk), qk, precision=lax.Precision.HIGHEST,
                               preferred_element_type=_F32)
        sel = _topk_select(gate, blk_iota, nb, axis=2)
        s_own = lax.dot_general(qb_ref[...], kn_ref[0].astype(_BF16), pv,
                                preferred_element_type=_F32) + bown_ref[...]
        m_all = m_ref[...]
        m_fin = jnp.maximum(jnp.max(s_own, axis=2, keepdims=True),
                            jnp.max(jnp.where(sel, m_all, -jnp.inf), axis=2, keepdims=True))
        w = jnp.where(sel, jnp.exp(m_all - m_fin), 0.0)
        p_own = jnp.exp(s_own - m_fin)
        denom = jnp.sum(w * l_ref[...], axis=2, keepdims=True) + jnp.sum(p_own, axis=2, keepdims=True)
        out = lax.dot_general(p_own.astype(_BF16), vn_ref[0].astype(_BF16), qk, preferred_element_type=_F32)

        def merge(jj, out):
            w_j = jnp.sum(jnp.where(blk_iota == jj, w, 0.0), axis=2, keepdims=True)
            return out + w_j * acc_ref[jj]

        o_ref[0] = lax.fori_loop(0, nb, merge, out) / denom


def _sample_bias_tables(slopes, n_t, tp, past_len):
    n_h = slopes.shape[0]
    key = jnp.arange(_MOBA_BLOCK, dtype=_F32)
    kbias = jnp.broadcast_to(slopes[:, None, None] * key[None, None, :], (n_h, tp, _MOBA_BLOCK))
    t = jnp.arange(tp)[:, None]
    r = jnp.arange(tp)[None, :]
    ok = jnp.logical_and(r <= t, r < n_t)
    bown = jnp.where(ok[None], -slopes[:, None, None] * (t - r).astype(_F32)[None], _NEG)
    slope3 = jnp.broadcast_to(slopes[:, None, None], (n_h, tp, 1))
    pos3 = jnp.broadcast_to(past_len + t.astype(_F32)[None], (n_h, tp, 1))
    return kbias.astype(_F32), bown.astype(_F32), slope3.astype(_F32), pos3.astype(_F32)


def _moba_sample(q, kn, vn, cache_kt, cache_vt, page_table, tables, *, li):
    n_seq, n_h, tp, hd = q.shape
    page_size = cache_kt.shape[4]
    n_pages = page_table.shape[1]
    assert _MOBA_BLOCK == 2 * page_size
    nb = n_pages // 2
    kern = _moba_sample_kernel
    tok = pl.BlockSpec((1, n_h, tp, hd), lambda n, j, pt: (n, 0, 0, 0))
    page = lambda half: pl.BlockSpec((1, 1, n_h, hd, page_size),
                                     lambda n, j, pt: (pt[n, 2 * j + half], li, 0, 0, 0))
    const = lambda arr: pl.BlockSpec(arr.shape, lambda n, j, pt: (0,) * arr.ndim)
    grid_spec = pltpu.PrefetchScalarGridSpec(
        num_scalar_prefetch=1,
        grid=(n_seq, nb),
        in_specs=[tok, tok, tok, page(0), page(1), page(0), page(1)] + [const(t) for t in tables],
        out_specs=tok,
        scratch_shapes=[
            pltpu.VMEM((n_h, tp, hd), _BF16),
            pltpu.VMEM((n_h, hd, nb), _F32),
            pltpu.VMEM((n_h, tp, nb), _F32),
            pltpu.VMEM((n_h, tp, nb), _F32),
            pltpu.VMEM((nb, n_h, tp, hd), _F32),
        ],
    )
    return pl.pallas_call(
        kern,
        grid_spec=grid_spec,
        out_shape=jax.ShapeDtypeStruct((n_seq, n_h, tp, hd), _F32),
        compiler_params=_cparams("parallel", "arbitrary"),
        name="moba_sample",
    )(page_table, q, kn, vn, cache_kt, cache_kt, cache_vt, cache_vt, *tables)


def _lru_gates(xc, wr_ref, br, wi_ref, bi, lam):
    bw = wr_ref.shape[1]
    xb = xc.astype(wr_ref.dtype)
    r_parts, i_parts = [], []
    for g in range(wr_ref.shape[0]):
        xg = xb[:, g * bw:(g + 1) * bw]
        r_parts.append(_mxu(xg, wr_ref[g]))
        i_parts.append(_mxu(xg, wi_ref[g]))
    r = jax.nn.sigmoid(jnp.concatenate(r_parts, axis=1) + br)
    i = jax.nn.sigmoid(jnp.concatenate(i_parts, axis=1) + bi)
    softplus_neg_lam = jnp.maximum(-lam, 0.0) + jnp.log1p(jnp.exp(-jnp.abs(lam)))
    log_a = (-_LRU_C * softplus_neg_lam) * r
    a = jnp.exp(log_a)
    bx = jnp.sqrt(1.0 - a * a) * (i * xc)
    return a, bx


def _scan_rows(a, b):
    n = a.shape[0]
    row = lax.broadcasted_iota(jnp.int32, a.shape, 0)
    s = 1
    while s < n:
        keep = row >= s
        a_sh = jnp.where(keep, pltpu.roll(a, s, 0), 1.0)
        b_sh = jnp.where(keep, pltpu.roll(b, s, 0), 0.0)
        b = a * b_sh + b
        a = a * a_sh
        s *= 2
    return a, b


def _out_proj(mix_a, mix_b, wout_ref, x):
    half = wout_ref.shape[0] // 2
    return x + _mxu(mix_a, wout_ref[0:half, :]) + _mxu(mix_b, wout_ref[half:2 * half, :])


def _even_out_prompt_kernel(xa_ref, ga_ref, gb_ref, o_ref, x_ref, cw_ref, cb_ref, wr_ref, br_ref, wi_ref,
                            bi_ref, lam_ref, wout_ref, y_ref, hlast_ref, ext_ref, h_ref):
    tile = xa_ref.shape[1]
    pre = _SUBLANES
    taps = cw_ref.shape[0]

    @pl.when(pl.program_id(1) == 0)
    def _():
        ext_ref[0:pre, :] = jnp.zeros((pre, ext_ref.shape[1]), _F32)
        h_ref[...] = jnp.zeros(h_ref.shape, _F32)

    xa = xa_ref[0]
    ext_ref[pre:pre + tile, :] = xa
    xc = cb_ref[...] + cw_ref[taps - 1:taps, :] * xa
    for k in range(taps - 1):
        off = pre - (taps - 1) + k
        xc = xc + cw_ref[k:k + 1, :] * ext_ref[off:off + tile, :]
    ext_ref[0:pre, :] = xa[tile - pre:tile, :]

    a, bx = _lru_gates(xc, wr_ref, br_ref[...], wi_ref, bi_ref[...], lam_ref[...])
    a_cum, h = _scan_rows(a, bx)
    h = h + a_cum * h_ref[...]
    h_ref[...] = h[tile - 1:tile, :]
    hlast_ref[0] = h[tile - 1:tile, :]

    y_ref[0] = _out_proj(h * _silu(ga_ref[0]), o_ref[0] * _silu(gb_ref[0]), wout_ref, x_ref[0])


def _even_out_prompt(z3, o, x, cw, cb, wr, br, wi, bi, lam, wout, *, tile, w_a):
    b, s, d = x.shape
    nc = w_a // d
    assert nc == 1
    row = lambda col: pl.BlockSpec((1, tile, d), lambda bi_, t: (bi_, t, col))
    full = lambda arr: pl.BlockSpec(arr.shape, lambda bi_, t: (0,) * arr.ndim)
    vec = pl.BlockSpec((1, d), lambda bi_, t: (0, 0))
    return pl.pallas_call(
        _even_out_prompt_kernel,
        grid=(b, s // tile),
        in_specs=[row(0), row(1), row(5), row(0), row(0), full(cw), vec, full(wr), vec, full(wi), vec, vec,
                  full(wout)],
        out_specs=[row(0), pl.BlockSpec((1, 1, d), lambda bi_, t: (bi_, 0, 0))],
        out_shape=[jax.ShapeDtypeStruct((b, s, d), _F32), jax.ShapeDtypeStruct((b, 1, d), _F32)],
        scratch_shapes=[pltpu.VMEM((_SUBLANES + tile, d), _F32), pltpu.VMEM((1, d), _F32)],
        compiler_params=_cparams("parallel", "arbitrary"),
        name="even_out_prompt",
    )(z3, z3, z3, o, x, cw, cb.reshape(1, d), wr, br.reshape(1, d), wi, bi.reshape(1, d), lam.reshape(1, d), wout)


def _even_out_sample_kernel(z_ref, o_ref, x_ref, pre_ref, h0_ref, cw_ref, cb_ref, wr_ref, br_ref, wi_ref,
                            bi_ref, lam_ref, wout_ref, y_ref, hlast_ref, cstate_ref, *, n_seq):
    d = x_ref.shape[1]
    rows = x_ref.shape[0]
    taps = cw_ref.shape[0]
    xa = z_ref[:, 0:d]
    ext = jnp.concatenate([pre_ref[...], xa], axis=0)
    xc = cb_ref[...]
    for k in range(taps):
        xc = xc + cw_ref[k:k + 1, :] * ext[k * n_seq:k * n_seq + rows, :]
    cstate_ref[...] = ext[rows:rows + (taps - 1) * n_seq, :]

    a, bx = _lru_gates(xc, wr_ref, br_ref[...], wi_ref, bi_ref[...], lam_ref[...])
    h = h0_ref[...]
    hs = []
    for t in range(rows // n_seq):
        h = a[t * n_seq:(t + 1) * n_seq, :] * h + bx[t * n_seq:(t + 1) * n_seq, :]
        hs.append(h)
    hlast_ref[...] = h
    ha = jnp.concatenate(hs, axis=0)
    y_ref[...] = _out_proj(ha * _silu(z_ref[:, d:2 * d]), o_ref[...] * _silu(z_ref[:, 5 * d:6 * d]), wout_ref,
                           x_ref[...])


def _even_out_sample(z, o, x, pre, h0, cw, cb, wr, br, wi, bi, lam, wout, *, n_seq):
    rows, d = x.shape
    args = (z, o, x, pre, h0, cw, cb.reshape(1, d), wr, br.reshape(1, d), wi, bi.reshape(1, d), lam.reshape(1, d),
            wout)
    full = lambda arr: pl.BlockSpec(arr.shape, lambda i: (0,) * arr.ndim)
    once = lambda arr: pl.BlockSpec(arr.shape, lambda i: (0,) * arr.ndim, pipeline_mode=pl.Buffered(1))
    outs = [jax.ShapeDtypeStruct((rows, d), _F32), jax.ShapeDtypeStruct((n_seq, d), _F32),
            jax.ShapeDtypeStruct(pre.shape, _F32)]
    return pl.pallas_call(
        functools.partial(_even_out_sample_kernel, n_seq=n_seq),
        grid=(1,),
        in_specs=[once(a) for a in args],
        out_specs=[full(o_) for o_ in outs],
        out_shape=outs,
        compiler_params=_cparams("arbitrary"),
        name="even_out_sample",
    )(*args)


def _pool_project(pooled, pw_ref, pb, ps):
    gw = pw_ref.shape[1]
    pooled = pooled.astype(pw_ref.dtype)
    parts = [_mxu(pooled[:, g * gw:(g + 1) * gw], pw_ref[g]) for g in range(pw_ref.shape[0])]
    return (jnp.concatenate(parts, axis=1) + pb) * ps


def _layernorm(x, g, b):
    xc = x - jnp.mean(x, axis=-1, keepdims=True)
    var = jnp.mean(xc * xc, axis=-1, keepdims=True)
    return xc * lax.rsqrt(var + _EPS) * g + b


def _odd_out_prompt_kernel(xc_ref, gc_ref, da_ref, db_ref, gd_ref, x_ref, pw_ref, pb_ref, ps_ref, dw_ref,
                           dbias_ref, lg_ref, lb_ref, wout_ref, nf_ref, y_ref, dstate_ref, pext_ref, uext_ref,
                           *, final_norm):
    tile = xc_ref.shape[1]
    d = xc_ref.shape[2]
    ppre = pext_ref.shape[0] - tile
    upre = uext_ref.shape[0] - tile
    taps = dw_ref.shape[0]
    t_idx = pl.program_id(1)

    @pl.when(t_idx == 0)
    def _():
        pext_ref[0:ppre, :] = jnp.zeros((ppre, d), _F32)
        uext_ref[0:upre, :] = jnp.zeros((upre, d), _F32)

    xc = xc_ref[0]
    pext_ref[ppre:ppre + tile, :] = xc
    pos = t_idx * tile + lax.broadcasted_iota(jnp.int32, (tile, 1), 0)
    gw = d // len(_POOL_WINDOWS)
    parts = []
    for g, win in enumerate(_POOL_WINDOWS):
        ssum = xc[:, g * gw:(g + 1) * gw]
        for sft in range(1, win):
            ssum = ssum + pext_ref[ppre - sft:ppre - sft + tile, g * gw:(g + 1) * gw]
        cnt = jnp.minimum(win, pos + 1).astype(_F32)
        parts.append(ssum / cnt)
    pooled = jnp.concatenate(parts, axis=1) - xc
    pext_ref[0:ppre, :] = xc[tile - ppre:tile, :]
    yc = _pool_project(pooled, pw_ref, pb_ref[...], ps_ref[...])

    u = da_ref[0] * jax.nn.sigmoid(db_ref[0])
    uext_ref[upre:upre + tile, :] = u
    dc = dbias_ref[...] + dw_ref[taps - 1:taps, :] * u
    for k in range(taps - 1):
        off = upre - (taps - 1) + k
        dc = dc + dw_ref[k:k + 1, :] * uext_ref[off:off + tile, :]
    uext_ref[0:upre, :] = u[tile - upre:tile, :]
    dstate_ref[0] = u[tile - upre:tile, :]
    yd = _silu(_layernorm(dc, lg_ref[...], lb_ref[...]))

    y = _out_proj(yc * _silu(gc_ref[0]), yd * _silu(gd_ref[0]), wout_ref, x_ref[0])
    if final_norm:
        y = _rms(y, nf_ref[...])
    y_ref[0] = y


def _odd_out_prompt(z3, x, pw, pb, ps, dw, dbias, lg, lb, wout, nf, *, tile, final_norm):
    b, s, d = x.shape
    ppre = 16
    upre = 32
    assert max(_POOL_WINDOWS) - 1 <= ppre and dw.shape[0] - 1 <= upre
    row = lambda col: pl.BlockSpec((1, tile, d), lambda bi_, t: (bi_, t, col))
    full = lambda arr: pl.BlockSpec(arr.shape, lambda bi_, t: (0,) * arr.ndim)
    vec = pl.BlockSpec((1, d), lambda bi_, t: (0, 0))
    return pl.pallas_call(
        functools.partial(_odd_out_prompt_kernel, final_norm=final_norm),
        grid=(b, s // tile),
        in_specs=[row(0), row(1), row(2), row(3), row(4), row(0), full(pw), vec, vec, full(dw), vec, vec, vec,
                  full(wout), vec],
        out_specs=[row(0), pl.BlockSpec((1, upre, d), lambda bi_, t: (bi_, 0, 0))],
        out_shape=[jax.ShapeDtypeStruct((b, s, d), _F32), jax.ShapeDtypeStruct((b, upre, d), _F32)],
        scratch_shapes=[pltpu.VMEM((ppre + tile, d), _F32), pltpu.VMEM((upre + tile, d), _F32)],
        compiler_params=_cparams("parallel", "arbitrary"),
        name="odd_out_prompt",
    )(z3, z3, z3, z3, z3, x, pw, pb.reshape(1, d), ps.reshape(1, d), dw, dbias.reshape(1, d), lg.reshape(1, d),
      lb.reshape(1, d), wout, nf.reshape(1, d))


def _odd_out_sample_kernel(z_ref, x_ref, ppre_ref, upre_ref, pw_ref, pb_ref, ps_ref, dw_ref, dbias_ref, lg_ref,
                           lb_ref, wout_ref, nf_ref, y_ref, pstate_ref, dstate_ref, *, n_seq, pos0, final_norm):
    rows, d = x_ref.shape
    taps = dw_ref.shape[0]
    pbuf = ppre_ref.shape[0] // n_seq
    xc = z_ref[:, 0:d]
    pext = jnp.concatenate([ppre_ref[...], xc], axis=0)
    pstate_ref[...] = pext[rows:rows + pbuf * n_seq, :]
    t_of_row = lax.broadcasted_iota(jnp.int32, (rows, 1), 0) // n_seq
    gw = d // len(_POOL_WINDOWS)
    parts = []
    for g, win in enumerate(_POOL_WINDOWS):
        ssum = xc[:, g * gw:(g + 1) * gw]
        for sft in range(1, win):
            lo = (pbuf - sft) * n_seq
            ssum = ssum + pext[lo:lo + rows, g * gw:(g + 1) * gw]
        cnt = jnp.minimum(win, pos0 + t_of_row + 1).astype(_F32)
        parts.append(ssum / cnt)
    pooled = jnp.concatenate(parts, axis=1) - xc
    yc = _pool_project(pooled, pw_ref, pb_ref[...], ps_ref[...])

    u = z_ref[:, 2 * d:3 * d] * jax.nn.sigmoid(z_ref[:, 3 * d:4 * d])
    uext = jnp.concatenate([upre_ref[...], u], axis=0)
    dstate_ref[...] = uext[rows:rows + (taps - 1) * n_seq, :]
    dc = dbias_ref[...]
    for k in range(taps):
        dc = dc + dw_ref[k:k + 1, :] * uext[k * n_seq:k * n_seq + rows, :]
    yd = _silu(_layernorm(dc, lg_ref[...], lb_ref[...]))

    y = _out_proj(yc * _silu(z_ref[:, d:2 * d]), yd * _silu(z_ref[:, 4 * d:5 * d]), wout_ref, x_ref[...])
    if final_norm:
        y = _rms(y, nf_ref[...])
    y_ref[...] = y


def _odd_out_sample(z, x, ppre, upre, pw, pb, ps, dw, dbias, lg, lb, wout, nf, *, n_seq, pos0, final_norm):
    rows, d = x.shape
    args = (z, x, ppre, upre, pw, pb.reshape(1, d), ps.reshape(1, d), dw, dbias.reshape(1, d), lg.reshape(1, d),
            lb.reshape(1, d), wout, nf.reshape(1, d))
    full = lambda arr: pl.BlockSpec(arr.shape, lambda i: (0,) * arr.ndim)
    once = lambda arr: pl.BlockSpec(arr.shape, lambda i: (0,) * arr.ndim, pipeline_mode=pl.Buffered(1))
    outs = [jax.ShapeDtypeStruct((rows, d), _F32), jax.ShapeDtypeStruct(ppre.shape, _F32),
            jax.ShapeDtypeStruct(upre.shape, _F32)]
    return pl.pallas_call(
        functools.partial(_odd_out_sample_kernel, n_seq=n_seq, pos0=pos0, final_norm=final_norm),
        grid=(1,),
        in_specs=[once(a) for a in args],
        out_specs=[full(o_) for o_ in outs],
        out_shape=outs,
        compiler_params=_cparams("arbitrary"),
        name="odd_out_sample",
    )(*args)


def _time_major(state):
    n, k, c = state.shape
    return jnp.transpose(state, (1, 0, 2)).reshape(k * n, c)


def _batch_major(state_tm, n):
    kn, c = state_tm.shape
    return jnp.transpose(state_tm.reshape(kn // n, n, c), (1, 0, 2))


def kernel(x_prompt, x_sample, cache_k, cache_v, page_table, state_lru_h, state_lru_conv, state_pool, state_dconv, norm_even, w_in_even, lru_conv_w, lru_conv_b, lru_wr, lru_br, lru_wi, lru_bi, lru_lambda, w_out_even, norm_odd, w_in_odd, pool_w, pool_b, pool_scale, dconv_w, dconv_b, dln_g, dln_b, w_out_odd, norm_final):
    bp, seq, d = x_prompt.shape
    n_seq, n_t, _ = x_sample.shape
    n_even, n_odd = w_in_even.shape[0], w_in_odd.shape[0]
    depth = n_even + n_odd
    n_phys, _, page_size, n_heads, hd = cache_k.shape
    w_b = n_heads * hd
    n_pages = page_table.shape[1]
    past_len = n_pages * page_size
    lru_taps = lru_conv_w.shape[1]
    pool_buf = state_pool.shape[2]
    dconv_taps = dconv_w.shape[1]
    assert w_b == d and seq % _MOBA_BLOCK == 0 and past_len % _MOBA_BLOCK == 0 and n_t <= _SUBLANES
    assert seq // _MOBA_BLOCK >= _MOBA_TOPK and past_len // _MOBA_BLOCK >= _MOBA_TOPK

    slopes = 2.0 ** (-8.0 * (jnp.arange(n_heads, dtype=_F32) + 1.0) / n_heads)
    tp = _BF16_ROWS
    tables = _sample_bias_tables(slopes, n_t, tp, float(past_len))
    to_bf16 = lambda w: w.astype(_BF16)
    keep = lambda w: w
    cache_kt = jnp.transpose(cache_k, (0, 1, 3, 4, 2))
    cache_vt = jnp.transpose(cache_v, (0, 1, 3, 4, 2))

    def tokens(zs, col):
        part = zs[:, col * d:(col + 1) * d].reshape(n_t, n_seq, n_heads, hd)
        return jnp.transpose(part, (1, 0, 2, 3))

    def head_major(x4):
        return jnp.pad(jnp.transpose(x4, (0, 2, 1, 3)), ((0, 0), (0, 0), (0, tp - n_t), (0, 0)))

    xp = x_prompt
    xs = jnp.transpose(x_sample, (1, 0, 2)).reshape(n_t * n_seq, d)
    tile_p = 256
    tm_p = 1024

    kp_l, vp_l, ks_l, vs_l = [], [], [], []
    hp_l, hs_l, cp_l, cs_l = [], [], [], []
    pp_l, ps_l, dp_l, ds_l = [], [], [], []
    for layer in range(depth):
        li = layer // 2
        last = layer == depth - 1
        if layer % 2 == 0:
            lru_w = lambda cast: (lru_conv_w[li], lru_conv_b[li], cast(lru_wr[li]), lru_br[li], cast(lru_wi[li]),
                                  lru_bi[li], lru_lambda[li], cast(w_out_even[li]))
            z = _in_proj(xp.reshape(bp * seq, d), norm_even[li], to_bf16(w_in_even[li]), tm=tm_p, tn=512)
            z3 = z.reshape(bp, seq, z.shape[1])
            o = _moba_prompt(z3, slopes, q_col=2, k_col=3, v_col=4, w_b=w_b, hd=hd)
            xp, h_last = _even_out_prompt(z3, o, xp, *lru_w(to_bf16), tile=tile_p, w_a=d)
            kp_l.append(z3[:, :, 3 * d:4 * d].reshape(bp, seq, n_heads, hd))
            vp_l.append(z3[:, :, 4 * d:5 * d].reshape(bp, seq, n_heads, hd))
            hp_l.append(h_last.reshape(bp, d))
            cp_l.append(z3[:, seq - (lru_taps - 1):, 0:d])
            zs = _in_proj(xs, norm_even[li], w_in_even[li], tm=n_t * n_seq, tn=512)
            kn, vn = tokens(zs, 3), tokens(zs, 4)
            os_ = _moba_sample(head_major(tokens(zs, 2)), head_major(kn), head_major(vn), cache_kt, cache_vt,
                               page_table, tables, li=li)
            os_tm = jnp.transpose(os_[:, :, :n_t], (2, 0, 1, 3)).reshape(n_t * n_seq, w_b)
            xs, hs_last, cs_tm = _even_out_sample(zs, os_tm, xs, _time_major(state_lru_conv[:, li]),
                                                  state_lru_h[:, li], *lru_w(keep), n_seq=n_seq)
            ks_l.append(kn)
            vs_l.append(vn)
            hs_l.append(hs_last)
            cs_l.append(_batch_major(cs_tm, n_seq))
        else:
            odd_w = lambda cast: (cast(pool_w[li]), pool_b[li], pool_scale[li], dconv_w[li], dconv_b[li], dln_g[li],
                                  dln_b[li], cast(w_out_odd[li]), norm_final)
            z = _in_proj(xp.reshape(bp * seq, d), norm_odd[li], to_bf16(w_in_odd[li]), tm=tm_p, tn=512)
            z3 = z.reshape(bp, seq, z.shape[1])
            xp, dstate = _odd_out_prompt(z3, xp, *odd_w(to_bf16), tile=tile_p, final_norm=last)
            pp_l.append(z3[:, seq - pool_buf:, 0:d])
            dp_l.append(dstate[:, dstate.shape[1] - (dconv_taps - 1):, :])
            zs = _in_proj(xs, norm_odd[li], w_in_odd[li], tm=n_t * n_seq, tn=512)
            xs, ps_tm, ds_tm = _odd_out_sample(zs, xs, _time_major(state_pool[:, li]),
                                               _time_major(state_dconv[:, li]), *odd_w(keep), n_seq=n_seq,
                                               pos0=past_len, final_norm=last)
            ps_l.append(_batch_major(ps_tm, n_seq))
            ds_l.append(_batch_major(ds_tm, n_seq))

    if depth % 2 == 1:
        raise NotImplementedError("final norm is fused into the last odd layer")
    y_prompt = xp
    y_sample = jnp.transpose(xs.reshape(n_t, n_seq, d), (1, 0, 2))
    stack = lambda xs_: jnp.stack(xs_, axis=1)
    return (y_prompt, y_sample, stack(kp_l), stack(vp_l), stack(ks_l), stack(vs_l),
            stack(hp_l), stack(hs_l), stack(cp_l), stack(cs_l),
            stack(pp_l), stack(ps_l), stack(dp_l), stack(ds_l))
```

```python
import functools

import jax
import jax.numpy as jnp
from jax import lax
from jax.experimental import pallas as pl
from jax.experimental.pallas import tpu as pltpu

_MOBA_BLOCK = 256
_MOBA_TOPK = 3
_MOBA_HEAD_GROUP = 4
_MOBA_BLOCKS_PER_TRIP = 4
_SAMPLE_BLOCKS_PER_STEP = 2
_N_HEADS = 16
_LRU_BLOCKS = 8
_LRU_C = 8.0
_POOL_WINDOWS = (2, 4, 8, 16)
_EPS = 1e-6
_NEG = -1e30

_LANES = 128
_SUBLANES = 8
_BF16_ROWS = 16
_LOG2E = 1.4426950408889634
_BIAS_TERMS = 3
_VMEM_LIMIT_BYTES = 56 * 1024 * 1024

_BF16 = jnp.bfloat16
_F32 = jnp.float32
_NT = (((1,), (1,)), ((), ()))


def _cparams(*sem):
    return pltpu.CompilerParams(dimension_semantics=sem, vmem_limit_bytes=_VMEM_LIMIT_BYTES)


def _mxu(x, w):
    if w.dtype == _BF16:
        return jnp.dot(x.astype(_BF16), w, preferred_element_type=_F32)
    return jnp.dot(x, w, precision=lax.Precision.HIGHEST, preferred_element_type=_F32)


def _silu(x):
    return x * jax.nn.sigmoid(x)


def _rms(x, g):
    return x * lax.rsqrt(jnp.mean(x * x, axis=-1, keepdims=True) + _EPS) * g


def _in_proj_kernel(x_ref, g_ref, w_ref, z_ref, h_ref):
    @pl.when(pl.program_id(1) == 0)
    def _():
        h_ref[...] = _rms(x_ref[...], g_ref[...]).astype(h_ref.dtype)

    z_ref[...] = _mxu(h_ref[...], w_ref[...])


def _in_proj(x, g, w, *, tm, tn):
    m, d = x.shape
    n = w.shape[1]
    assert m % tm == 0 and n % tn == 0
    return pl.pallas_call(
        _in_proj_kernel,
        grid=(m // tm, n // tn),
        in_specs=[
            pl.BlockSpec((tm, d), lambda i, j: (i, 0)),
            pl.BlockSpec((1, d), lambda i, j: (0, 0)),
            pl.BlockSpec((d, tn), lambda i, j: (0, j)),
        ],
        out_specs=pl.BlockSpec((tm, tn), lambda i, j: (i, j)),
        out_shape=jax.ShapeDtypeStruct((m, n), _F32),
        scratch_shapes=[pltpu.VMEM((tm, d), w.dtype)],
        compiler_params=_cparams("parallel", "arbitrary"),
        name="in_proj",
    )(x, g.reshape(1, d), w)


def _topk_select(gate, blk_iota, nblk, axis):
    sel = jnp.zeros(gate.shape, jnp.bool_)
    g = gate
    for _ in range(_MOBA_TOPK):
        mx = jnp.max(g, axis=axis, keepdims=True)
        first = jnp.min(jnp.where(g == mx, blk_iota, nblk), axis=axis, keepdims=True)
        pick = blk_iota == first
        sel = jnp.logical_or(sel, pick)
        g = jnp.where(pick, -jnp.inf, g)
    return sel


def _moba_prompt_kernel(slopes_ref, q_ref, k_ref, v_ref, o_ref,
                        kb_ref, vt_ref, km_ref, qb_ref, pen_ref, acc_ref, *, hd):
    blk = _MOBA_BLOCK
    hp = pl.program_id(1)
    qi = pl.program_id(2)
    nblk = km_ref.shape[0]
    width = q_ref.shape[2]
    n_h = width // hd
    kw = kb_ref.shape[2]
    va = vt_ref.shape[2]

    @pl.when(qi == 0)
    def _():
        ones_rows = (lax.broadcasted_iota(jnp.int32, (va - hd, blk), 0) == 0).astype(_F32)
        lane = lax.broadcasted_iota(jnp.int32, (blk, kw), 1)
        key_r = lax.broadcasted_iota(jnp.int32, (blk, kw), 0).astype(_F32)
        feat = jnp.where(jnp.logical_and(lane >= hd, lane < hd + _BIAS_TERMS), key_r, 0.0)
        for j in range(nblk):
            kj = k_ref[0, j * blk:(j + 1) * blk, :]
            km_ref[j:j + 1, :] = jnp.sum(kj, axis=0, keepdims=True) * (1.0 / blk)
            vt = v_ref[0, j * blk:(j + 1) * blk, :].T
            for hh in range(n_h):
                lo = hh * hd // kw * kw
                chunk = kj[:, lo:lo + kw]
                if hh * hd != lo:
                    chunk = pltpu.roll(chunk, kw - (hh * hd - lo), 1)
                kb_ref[hh, j * blk:(j + 1) * blk, :] = jnp.where(lane < hd, chunk, feat).astype(_BF16)
                vt_ref[j, hh] = jnp.concatenate([vt[hh * hd:(hh + 1) * hd, :], ones_rows], axis=0).astype(_BF16)

    qt = q_ref[0].T
    term_row = lax.broadcasted_iota(jnp.int32, (kw - hd, blk), 0)
    d_iota = lax.broadcasted_iota(jnp.int32, (width, blk), 0)
    blk_iota = lax.broadcasted_iota(jnp.int32, (nblk, blk), 0)
    key_i = lax.broadcasted_iota(jnp.int32, (blk, blk), 0)
    qry_i = lax.broadcasted_iota(jnp.int32, (blk, blk), 1)
    past = blk_iota < qi
    q_start = pl.multiple_of(qi * blk, blk)
    slopes = [slopes_ref[hp * n_h + hh] * _LOG2E for hh in range(n_h)]

    m_init = []
    for hh in range(n_h):
        qm = jnp.where(jnp.logical_and(d_iota >= hh * hd, d_iota < (hh + 1) * hd), qt, 0.0)
        gate = jnp.dot(km_ref[...], qm, precision=lax.Precision.HIGHEST, preferred_element_type=_F32)
        sel = _topk_select(jnp.where(past, gate, _NEG), blk_iota, nblk, axis=0)
        pen_ref[hh] = jnp.where(jnp.logical_and(sel, past), 0.0, _NEG)
        rest = jnp.full((kw - hd, blk), slopes[hh], _F32)
        terms = jnp.zeros((kw - hd, blk), _F32)
        for i in range(_BIAS_TERMS):
            part = rest.astype(_BF16).astype(_F32)
            terms = jnp.where(term_row == i, part, terms)
            rest = rest - part
        qb = jnp.concatenate([qt[hh * hd:(hh + 1) * hd, :] * (hd ** -0.5 * _LOG2E), terms], axis=0).astype(_BF16)
        qb_ref[hh] = qb
        s = jnp.dot(kb_ref[hh, pl.ds(q_start, blk), :], qb, preferred_element_type=_F32)
        s = jnp.where(key_i <= qry_i, s, _NEG)
        m0 = jnp.max(s, axis=0, keepdims=True)
        m_init.append(m0)
        acc_ref[hh] = jnp.dot(vt_ref[qi, hh], jnp.exp2(s - m0).astype(_BF16), preferred_element_type=_F32)

    grp = _MOBA_BLOCKS_PER_TRIP

    def body(jj, ms):
        j0 = grp * jj
        rows = pl.ds(pl.multiple_of(j0 * blk, grp * blk), grp * blk)
        sg = [jnp.dot(kb_ref[hh, rows, :], qb_ref[hh], preferred_element_type=_F32) for hh in range(n_h)]
        new_m = []
        for hh in range(n_h):
            pen = pen_ref[hh]
            ts, offs = [], []
            for u in range(grp):
                pen_u = jnp.sum(jnp.where(blk_iota == j0 + u, pen, 0.0), axis=0, keepdims=True)
                offs.append(pen_u - slopes[hh] * ((qi - j0 - u) * blk).astype(_F32))
                ts.append(sg[hh][u * blk:(u + 1) * blk, :])
            m_new = ms[hh]
            for u in range(grp):
                m_new = jnp.maximum(m_new, jnp.max(ts[u], axis=0, keepdims=True) + offs[u])
            alpha = jnp.exp2(ms[hh] - m_new)
            pv = None
            for u in range(grp):
                pu = jnp.dot(vt_ref[j0 + u, hh], jnp.exp2(ts[u] - (m_new - offs[u])).astype(_BF16),
                             preferred_element_type=_F32)
                pv = pu if pv is None else pv + pu
            acc_ref[hh] = alpha * acc_ref[hh] + pv
            new_m.append(m_new)
        return tuple(new_m)

    lax.fori_loop(0, (qi + grp - 1) // grp, body, tuple(m_init))
    outs = []
    for hh in range(n_h):
        acc = acc_ref[hh]
        outs.append(acc[0:hd, :] / acc[hd:hd + 1, :])
    o_ref[0] = jnp.concatenate(outs, axis=0).T


def _moba_prompt(z3, slopes, *, q_col, k_col, v_col, w_b, hd):
    b, s, _ = z3.shape
    width = _MOBA_HEAD_GROUP * hd
    n_h = width // hd
    groups = w_b // width
    nblk = s // _MOBA_BLOCK
    assert nblk % _MOBA_BLOCKS_PER_TRIP == 0 and w_b % width == 0
    kern = functools.partial(_moba_prompt_kernel, hd=hd)
    resident = lambda col: pl.BlockSpec((1, s, width), lambda bi, hp, qi, sl: (bi, 0, col * groups + hp),
                                        pipeline_mode=pl.Buffered(1))
    grid_spec = pltpu.PrefetchScalarGridSpec(
        num_scalar_prefetch=1,
        grid=(b, groups, nblk),
        in_specs=[
            pl.BlockSpec((1, _MOBA_BLOCK, width), lambda bi, hp, qi, sl: (bi, qi, q_col * groups + hp)),
            resident(k_col),
            resident(v_col),
        ],
        out_specs=pl.BlockSpec((1, _MOBA_BLOCK, width), lambda bi, hp, qi, sl: (bi, qi, hp)),
        scratch_shapes=[
            pltpu.VMEM((n_h, s, _LANES), _BF16),
            pltpu.VMEM((nblk, n_h, hd + _BF16_ROWS, _MOBA_BLOCK), _BF16),
            pltpu.VMEM((nblk, width), _F32),
            pltpu.VMEM((n_h, _LANES, _MOBA_BLOCK), _BF16),
            pltpu.VMEM((n_h, nblk, _MOBA_BLOCK), _F32),
            pltpu.VMEM((n_h, hd + _BF16_ROWS, _MOBA_BLOCK), _F32),
        ],
    )
    return pl.pallas_call(
        kern,
        grid_spec=grid_spec,
        out_shape=jax.ShapeDtypeStruct((b, s, w_b), _F32),
        compiler_params=_cparams("parallel", "parallel", "arbitrary"),
        name="moba_prompt",
    )(slopes, z3, z3, z3)


def _moba_sample_kernel(pt_ref, q_ref, kn_ref, vn_ref, *refs):
    n_pg = 4 * _SAMPLE_BLOCKS_PER_STEP
    page_refs = refs[:n_pg]
    kbias_ref, bown_ref, slope_ref, tpos_ref, o_ref, qb_ref, ks_ref, m_ref, l_ref, acc_ref = refs[n_pg:]
    blk = _MOBA_BLOCK
    step = pl.program_id(1)
    n_steps = pl.num_programs(1)
    _, n_h, tp, hd = q_ref.shape
    nb = m_ref.shape[2]
    blk_iota = lax.broadcasted_iota(jnp.int32, (n_h, tp, nb), 2)
    ks_iota = lax.broadcasted_iota(jnp.int32, (n_h, hd, nb), 2)
    qk = (((2,), (1,)), ((0,), (0,)))
    pv = (((2,), (2,)), ((0,), (0,)))

    @pl.when(step == 0)
    def _():
        qb_ref[...] = (q_ref[0] * (hd ** -0.5)).astype(_BF16)
        m_ref[...] = jnp.zeros(m_ref.shape, _F32)
        l_ref[...] = jnp.zeros(l_ref.shape, _F32)
        ks_ref[...] = jnp.zeros(ks_ref.shape, _F32)

    for u in range(_SAMPLE_BLOCKS_PER_STEP):
        j = step * _SAMPLE_BLOCKS_PER_STEP + u
        k0_ref, k1_ref, v0_ref, v1_ref = page_refs[4 * u:4 * u + 4]
        k = jnp.concatenate([k0_ref[0, 0], k1_ref[0, 0]], axis=2)
        v = jnp.concatenate([v0_ref[0, 0], v1_ref[0, 0]], axis=2)
        ks_ref[...] = jnp.where(ks_iota == j, jnp.sum(k, axis=2, keepdims=True), ks_ref[...])
        s = lax.dot_general(qb_ref[...], k.astype(_BF16), qk, preferred_element_type=_F32) + kbias_ref[...]
        m_loc = jnp.max(s, axis=2, keepdims=True)
        p = jnp.exp(s - m_loc)
        l_loc = jnp.sum(p, axis=2, keepdims=True)
        acc_ref[j] = lax.dot_general(p.astype(_BF16), v.astype(_BF16), pv, preferred_element_type=_F32)
        m_blk = m_loc - slope_ref[...] * (tpos_ref[...] - (j * blk).astype(_F32))
        m_ref[...] = jnp.where(blk_iota == j, m_blk, m_ref[...])
        l_ref[...] = jnp.where(blk_iota == j, l_loc, l_ref[...])

    @pl.when(step == n_steps - 1)
    def _():
        gate = lax.dot_general(q_ref[0], ks_ref[...] * (1.0 / blk), qk, precision=lax.Precision.HIGHEST,
                               preferred_element_type=_F32)
        sel = _topk_select(gate, blk_iota, nb, axis=2)
        s_own = lax.dot_general(qb_ref[...], kn_ref[0].astype(_BF16), pv,
                                preferred_element_type=_F32) + bown_ref[...]
        m_all = m_ref[...]
        m_fin = jnp.maximum(jnp.max(s_own, axis=2, keepdims=True),
                            jnp.max(jnp.where(sel, m_all, -jnp.inf), axis=2, keepdims=True))
        w = jnp.where(sel, jnp.exp(m_all - m_fin), 0.0)
        p_own = jnp.exp(s_own - m_fin)
        denom = jnp.sum(w * l_ref[...], axis=2, keepdims=True) + jnp.sum(p_own, axis=2, keepdims=True)
        out = lax.dot_general(p_own.astype(_BF16), vn_ref[0].astype(_BF16), qk, preferred_element_type=_F32)

        def merge(jj, out):
            w_j = jnp.sum(jnp.where(blk_iota == jj, w, 0.0), axis=2, keepdims=True)
            return out + w_j * acc_ref[jj]

        o_ref[0] = lax.fori_loop(0, nb, merge, out) / denom


def _sample_bias_tables(slopes, n_t, tp, past_len):
    n_h = slopes.shape[0]
    key = jnp.arange(_MOBA_BLOCK, dtype=_F32)
    kbias = jnp.broadcast_to(slopes[:, None, None] * key[None, None, :], (n_h, tp, _MOBA_BLOCK))
    t = jnp.arange(tp)[:, None]
    r = jnp.arange(tp)[None, :]
    ok = jnp.logical_and(r <= t, r < n_t)
    bown = jnp.where(ok[None], -slopes[:, None, None] * (t - r).astype(_F32)[None], _NEG)
    slope3 = jnp.broadcast_to(slopes[:, None, None], (n_h, tp, 1))
    pos3 = jnp.broadcast_to(past_len + t.astype(_F32)[None], (n_h, tp, 1))
    return kbias.astype(_F32), bown.astype(_F32), slope3.astype(_F32), pos3.astype(_F32)


def _moba_sample(q, kn, vn, cache_kt, cache_vt, page_table, tables, *, li):
    n_seq, n_h, tp, hd = q.shape
    page_size = cache_kt.shape[4]
    n_pages = page_table.shape[1]
    assert _MOBA_BLOCK == 2 * page_size
    nb = n_pages // 2
    per_step = _SAMPLE_BLOCKS_PER_STEP
    assert nb % per_step == 0
    kern = _moba_sample_kernel
    tok = pl.BlockSpec((1, n_h, tp, hd), lambda n, j, pt: (n, 0, 0, 0))
    page = lambda i: pl.BlockSpec((1, 1, n_h, hd, page_size),
                                  lambda n, j, pt: (pt[n, 2 * per_step * j + i], li, 0, 0, 0))
    const = lambda arr: pl.BlockSpec(arr.shape, lambda n, j, pt: (0,) * arr.ndim)
    pages, operands = [], []
    for u in range(per_step):
        pages += [page(2 * u), page(2 * u + 1), page(2 * u), page(2 * u + 1)]
        operands += [cache_kt, cache_kt, cache_vt, cache_vt]
    grid_spec = pltpu.PrefetchScalarGridSpec(
        num_scalar_prefetch=1,
        grid=(n_seq, nb // per_step),
        in_specs=[tok, tok, tok] + pages + [const(t) for t in tables],
        out_specs=tok,
        scratch_shapes=[
            pltpu.VMEM((n_h, tp, hd), _BF16),
            pltpu.VMEM((n_h, hd, nb), _F32),
            pltpu.VMEM((n_h, tp, nb), _F32),
            pltpu.VMEM((n_h, tp, nb), _F32),
            pltpu.VMEM((nb, n_h, tp, hd), _F32),
        ],
    )
    return pl.pallas_call(
        kern,
        grid_spec=grid_spec,
        out_shape=jax.ShapeDtypeStruct((n_seq, n_h, tp, hd), _F32),
        compiler_params=_cparams("parallel", "arbitrary"),
        name="moba_sample",
    )(page_table, q, kn, vn, *operands, *tables)


def _lru_gates(xc, wr_ref, br, wi_ref, bi, lam):
    bw = wr_ref.shape[1]
    xb = xc.astype(wr_ref.dtype)
    r_parts, i_parts = [], []
    for g in range(wr_ref.shape[0]):
        xg = xb[:, g * bw:(g + 1) * bw]
        r_parts.append(_mxu(xg, wr_ref[g]))
        i_parts.append(_mxu(xg, wi_ref[g]))
    r = jax.nn.sigmoid(jnp.concatenate(r_parts, axis=1) + br)
    i = jax.nn.sigmoid(jnp.concatenate(i_parts, axis=1) + bi)
    softplus_neg_lam = jnp.maximum(-lam, 0.0) + jnp.log1p(jnp.exp(-jnp.abs(lam)))
    log_a = (-_LRU_C * softplus_neg_lam) * r
    a = jnp.exp(log_a)
    bx = jnp.sqrt(1.0 - a * a) * (i * xc)
    return a, bx


def _scan_rows(a, b):
    n = a.shape[0]
    row = lax.broadcasted_iota(jnp.int32, a.shape, 0)
    s = 1
    while s < n:
        keep = row >= s
        a_sh = jnp.where(keep, pltpu.roll(a, s, 0), 1.0)
        b_sh = jnp.where(keep, pltpu.roll(b, s, 0), 0.0)
        b = a * b_sh + b
        a = a * a_sh
        s *= 2
    return a, b


def _out_proj(mix_a, mix_b, wout_ref, x):
    half = wout_ref.shape[0] // 2
    return x + _mxu(mix_a, wout_ref[0:half, :]) + _mxu(mix_b, wout_ref[half:2 * half, :])


def _even_out_prompt_kernel(xa_ref, ga_ref, gb_ref, o_ref, x_ref, cw_ref, cb_ref, wr_ref, br_ref, wi_ref,
                            bi_ref, lam_ref, wout_ref, y_ref, hlast_ref, ext_ref, h_ref):
    tile = xa_ref.shape[1]
    pre = _SUBLANES
    taps = cw_ref.shape[0]

    @pl.when(pl.program_id(1) == 0)
    def _():
        ext_ref[0:pre, :] = jnp.zeros((pre, ext_ref.shape[1]), _F32)
        h_ref[...] = jnp.zeros(h_ref.shape, _F32)

    xa = xa_ref[0]
    ext_ref[pre:pre + tile, :] = xa
    xc = cb_ref[...] + cw_ref[taps - 1:taps, :] * xa
    for k in range(taps - 1):
        off = pre - (taps - 1) + k
        xc = xc + cw_ref[k:k + 1, :] * ext_ref[off:off + tile, :]
    ext_ref[0:pre, :] = xa[tile - pre:tile, :]

    a, bx = _lru_gates(xc, wr_ref, br_ref[...], wi_ref, bi_ref[...], lam_ref[...])
    a_cum, h = _scan_rows(a, bx)
    h = h + a_cum * h_ref[...]
    h_ref[...] = h[tile - 1:tile, :]
    hlast_ref[0] = h[tile - 1:tile, :]

    y_ref[0] = _out_proj(h * _silu(ga_ref[0]), o_ref[0] * _silu(gb_ref[0]), wout_ref, x_ref[0])


def _even_out_prompt(z3, o, x, cw, cb, wr, br, wi, bi, lam, wout, *, tile, w_a):
    b, s, d = x.shape
    nc = w_a // d
    assert nc == 1
    row = lambda col: pl.BlockSpec((1, tile, d), lambda bi_, t: (bi_, t, col))
    full = lambda arr: pl.BlockSpec(arr.shape, lambda bi_, t: (0,) * arr.ndim)
    vec = pl.BlockSpec((1, d), lambda bi_, t: (0, 0))
    return pl.pallas_call(
        _even_out_prompt_kernel,
        grid=(b, s // tile),
        in_specs=[row(0), row(1), row(5), row(0), row(0), full(cw), vec, full(wr), vec, full(wi), vec, vec,
                  full(wout)],
        out_specs=[row(0), pl.BlockSpec((1, 1, d), lambda bi_, t: (bi_, 0, 0))],
        out_shape=[jax.ShapeDtypeStruct((b, s, d), _F32), jax.ShapeDtypeStruct((b, 1, d), _F32)],
        scratch_shapes=[pltpu.VMEM((_SUBLANES + tile, d), _F32), pltpu.VMEM((1, d), _F32)],
        compiler_params=_cparams("parallel", "arbitrary"),
        name="even_out_prompt",
    )(z3, z3, z3, o, x, cw, cb.reshape(1, d), wr, br.reshape(1, d), wi, bi.reshape(1, d), lam.reshape(1, d), wout)


def _even_out_sample_kernel(z_ref, o_ref, x_ref, pre_ref, h0_ref, cw_ref, cb_ref, wr_ref, br_ref, wi_ref,
                            bi_ref, lam_ref, wout_ref, y_ref, hlast_ref, cstate_ref, *, n_seq):
    d = x_ref.shape[1]
    rows = x_ref.shape[0]
    taps = cw_ref.shape[0]
    xa = z_ref[:, 0:d]
    ext = jnp.concatenate([pre_ref[...], xa], axis=0)
    xc = cb_ref[...]
    for k in range(taps):
        xc = xc + cw_ref[k:k + 1, :] * ext[k * n_seq:k * n_seq + rows, :]
    cstate_ref[...] = ext[rows:rows + (taps - 1) * n_seq, :]

    a, bx = _lru_gates(xc, wr_ref, br_ref[...], wi_ref, bi_ref[...], lam_ref[...])
    h = h0_ref[...]
    hs = []
    for t in range(rows // n_seq):
        h = a[t * n_seq:(t + 1) * n_seq, :] * h + bx[t * n_seq:(t + 1) * n_seq, :]
        hs.append(h)
    hlast_ref[...] = h
    ha = jnp.concatenate(hs, axis=0)
    y_ref[...] = _out_proj(ha * _silu(z_ref[:, d:2 * d]), o_ref[...] * _silu(z_ref[:, 5 * d:6 * d]), wout_ref,
                           x_ref[...])


def _even_out_sample(z, o, x, pre, h0, cw, cb, wr, br, wi, bi, lam, wout, *, n_seq):
    rows, d = x.shape
    args = (z, o, x, pre, h0, cw, cb.reshape(1, d), wr, br.reshape(1, d), wi, bi.reshape(1, d), lam.reshape(1, d),
            wout)
    full = lambda arr: pl.BlockSpec(arr.shape, lambda i: (0,) * arr.ndim)
    once = lambda arr: pl.BlockSpec(arr.shape, lambda i: (0,) * arr.ndim, pipeline_mode=pl.Buffered(1))
    outs = [jax.ShapeDtypeStruct((rows, d), _F32), jax.ShapeDtypeStruct((n_seq, d), _F32),
            jax.ShapeDtypeStruct(pre.shape, _F32)]
    return pl.pallas_call(
        functools.partial(_even_out_sample_kernel, n_seq=n_seq),
        grid=(1,),
        in_specs=[once(a) for a in args],
        out_specs=[full(o_) for o_ in outs],
        out_shape=outs,
        compiler_params=_cparams("arbitrary"),
        name="even_out_sample",
    )(*args)


def _pool_project(pooled, pw_ref, pb, ps):
    gw = pw_ref.shape[1]
    pooled = pooled.astype(pw_ref.dtype)
    parts = [_mxu(pooled[:, g * gw:(g + 1) * gw], pw_ref[g]) for g in range(pw_ref.shape[0])]
    return (jnp.concatenate(parts, axis=1) + pb) * ps


def _layernorm(x, g, b):
    xc = x - jnp.mean(x, axis=-1, keepdims=True)
    var = jnp.mean(xc * xc, axis=-1, keepdims=True)
    return xc * lax.rsqrt(var + _EPS) * g + b


def _odd_out_prompt_kernel(xc_ref, gc_ref, da_ref, db_ref, gd_ref, x_ref, pw_ref, pb_ref, ps_ref, dw_ref,
                           dbias_ref, lg_ref, lb_ref, wout_ref, nf_ref, y_ref, dstate_ref, pext_ref, uext_ref,
                           *, final_norm):
    tile = xc_ref.shape[1]
    d = xc_ref.shape[2]
    ppre = pext_ref.shape[0] - tile
    upre = uext_ref.shape[0] - tile
    taps = dw_ref.shape[0]
    t_idx = pl.program_id(1)

    @pl.when(t_idx == 0)
    def _():
        pext_ref[0:ppre, :] = jnp.zeros((ppre, d), _F32)
        uext_ref[0:upre, :] = jnp.zeros((upre, d), _F32)

    xc = xc_ref[0]
    pext_ref[ppre:ppre + tile, :] = xc
    pos = t_idx * tile + lax.broadcasted_iota(jnp.int32, (tile, 1), 0)
    gw = d // len(_POOL_WINDOWS)
    parts = []
    for g, win in enumerate(_POOL_WINDOWS):
        ssum = xc[:, g * gw:(g + 1) * gw]
        for sft in range(1, win):
            ssum = ssum + pext_ref[ppre - sft:ppre - sft + tile, g * gw:(g + 1) * gw]
        cnt = jnp.minimum(win, pos + 1).astype(_F32)
        parts.append(ssum / cnt)
    pooled = jnp.concatenate(parts, axis=1) - xc
    pext_ref[0:ppre, :] = xc[tile - ppre:tile, :]
    yc = _pool_project(pooled, pw_ref, pb_ref[...], ps_ref[...])

    u = da_ref[0] * jax.nn.sigmoid(db_ref[0])
    uext_ref[upre:upre + tile, :] = u
    sub = _SUBLANES
    dc = dbias_ref[...]
    for b in range(min(sub, taps)):
        inner = None
        for s in range(b, taps, sub):
            lo = upre - sub - (s - b)
            term = dw_ref[taps - 1 - s:taps - s, :] * uext_ref[lo:lo + tile + sub, :]
            inner = term if inner is None else inner + term
        dc = dc + inner[sub - b:sub - b + tile, :]
    uext_ref[0:upre, :] = u[tile - upre:tile, :]
    dstate_ref[0] = u[tile - upre:tile, :]
    yd = _silu(_layernorm(dc, lg_ref[...], lb_ref[...]))

    y = _out_proj(yc * _silu(gc_ref[0]), yd * _silu(gd_ref[0]), wout_ref, x_ref[0])
    if final_norm:
        y = _rms(y, nf_ref[...])
    y_ref[0] = y


def _odd_out_prompt(z3, x, pw, pb, ps, dw, dbias, lg, lb, wout, nf, *, tile, final_norm):
    b, s, d = x.shape
    ppre = 16
    upre = 32
    assert max(_POOL_WINDOWS) - 1 <= ppre
    assert _SUBLANES + (dw.shape[0] - 1) // _SUBLANES * _SUBLANES <= upre
    row = lambda col: pl.BlockSpec((1, tile, d), lambda bi_, t: (bi_, t, col))
    full = lambda arr: pl.BlockSpec(arr.shape, lambda bi_, t: (0,) * arr.ndim)
    vec = pl.BlockSpec((1, d), lambda bi_, t: (0, 0))
    return pl.pallas_call(
        functools.partial(_odd_out_prompt_kernel, final_norm=final_norm),
        grid=(b, s // tile),
        in_specs=[row(0), row(1), row(2), row(3), row(4), row(0), full(pw), vec, vec, full(dw), vec, vec, vec,
                  full(wout), vec],
        out_specs=[row(0), pl.BlockSpec((1, upre, d), lambda bi_, t: (bi_, 0, 0))],
        out_shape=[jax.ShapeDtypeStruct((b, s, d), _F32), jax.ShapeDtypeStruct((b, upre, d), _F32)],
        scratch_shapes=[pltpu.VMEM((ppre + tile, d), _F32), pltpu.VMEM((upre + tile, d), _F32)],
        compiler_params=_cparams("parallel", "arbitrary"),
        name="odd_out_prompt",
    )(z3, z3, z3, z3, z3, x, pw, pb.reshape(1, d), ps.reshape(1, d), dw, dbias.reshape(1, d), lg.reshape(1, d),
      lb.reshape(1, d), wout, nf.reshape(1, d))


def _odd_out_sample_kernel(z_ref, x_ref, ppre_ref, upre_ref, pw_ref, pb_ref, ps_ref, dw_ref, dbias_ref, lg_ref,
                           lb_ref, wout_ref, nf_ref, y_ref, pstate_ref, dstate_ref, *, n_seq, pos0, final_norm):
    rows, d = x_ref.shape
    taps = dw_ref.shape[0]
    pbuf = ppre_ref.shape[0] // n_seq
    xc = z_ref[:, 0:d]
    pext = jnp.concatenate([ppre_ref[...], xc], axis=0)
    pstate_ref[...] = pext[rows:rows + pbuf * n_seq, :]
    t_of_row = lax.broadcasted_iota(jnp.int32, (rows, 1), 0) // n_seq
    gw = d // len(_POOL_WINDOWS)
    parts = []
    for g, win in enumerate(_POOL_WINDOWS):
        ssum = xc[:, g * gw:(g + 1) * gw]
        for sft in range(1, win):
            lo = (pbuf - sft) * n_seq
            ssum = ssum + pext[lo:lo + rows, g * gw:(g + 1) * gw]
        cnt = jnp.minimum(win, pos0 + t_of_row + 1).astype(_F32)
        parts.append(ssum / cnt)
    pooled = jnp.concatenate(parts, axis=1) - xc
    yc = _pool_project(pooled, pw_ref, pb_ref[...], ps_ref[...])

    u = z_ref[:, 2 * d:3 * d] * jax.nn.sigmoid(z_ref[:, 3 * d:4 * d])
    uext = jnp.concatenate([upre_ref[...], u], axis=0)
    dstate_ref[...] = uext[rows:rows + (taps - 1) * n_seq, :]
    dc = dbias_ref[...]
    for k in range(taps):
        dc = dc + dw_ref[k:k + 1, :] * uext[k * n_seq:k * n_seq + rows, :]
    yd = _silu(_layernorm(dc, lg_ref[...], lb_ref[...]))

    y = _out_proj(yc * _silu(z_ref[:, d:2 * d]), yd * _silu(z_ref[:, 4 * d:5 * d]), wout_ref, x_ref[...])
    if final_norm:
        y = _rms(y, nf_ref[...])
    y_ref[...] = y


def _odd_out_sample(z, x, ppre, upre, pw, pb, ps, dw, dbias, lg, lb, wout, nf, *, n_seq, pos0, final_norm):
    rows, d = x.shape
    args = (z, x, ppre, upre, pw, pb.reshape(1, d), ps.reshape(1, d), dw, dbias.reshape(1, d), lg.reshape(1, d),
            lb.reshape(1, d), wout, nf.reshape(1, d))
    full = lambda arr: pl.BlockSpec(arr.shape, lambda i: (0,) * arr.ndim)
    once = lambda arr: pl.BlockSpec(arr.shape, lambda i: (0,) * arr.ndim, pipeline_mode=pl.Buffered(1))
    outs = [jax.ShapeDtypeStruct((rows, d), _F32), jax.ShapeDtypeStruct(ppre.shape, _F32),
            jax.ShapeDtypeStruct(upre.shape, _F32)]
    return pl.pallas_call(
        functools.partial(_odd_out_sample_kernel, n_seq=n_seq, pos0=pos0, final_norm=final_norm),
        grid=(1,),
        in_specs=[once(a) for a in args],
        out_specs=[full(o_) for o_ in outs],
        out_shape=outs,
        compiler_params=_cparams("arbitrary"),
        name="odd_out_sample",
    )(*args)


def _time_major(state):
    n, k, c = state.shape
    return jnp.transpose(state, (1, 0, 2)).reshape(k * n, c)


def _batch_major(state_tm, n):
    kn, c = state_tm.shape
    return jnp.transpose(state_tm.reshape(kn // n, n, c), (1, 0, 2))


def kernel(x_prompt, x_sample, cache_k, cache_v, page_table, state_lru_h, state_lru_conv, state_pool, state_dconv, norm_even, w_in_even, lru_conv_w, lru_conv_b, lru_wr, lru_br, lru_wi, lru_bi, lru_lambda, w_out_even, norm_odd, w_in_odd, pool_w, pool_b, pool_scale, dconv_w, dconv_b, dln_g, dln_b, w_out_odd, norm_final):
    bp, seq, d = x_prompt.shape
    n_seq, n_t, _ = x_sample.shape
    n_even, n_odd = w_in_even.shape[0], w_in_odd.shape[0]
    depth = n_even + n_odd
    n_phys, _, page_size, n_heads, hd = cache_k.shape
    w_b = n_heads * hd
    n_pages = page_table.shape[1]
    past_len = n_pages * page_size
    lru_taps = lru_conv_w.shape[1]
    pool_buf = state_pool.shape[2]
    dconv_taps = dconv_w.shape[1]
    assert w_b == d and seq % _MOBA_BLOCK == 0 and past_len % _MOBA_BLOCK == 0 and n_t <= _SUBLANES
    assert seq // _MOBA_BLOCK >= _MOBA_TOPK and past_len // _MOBA_BLOCK >= _MOBA_TOPK

    slopes = 2.0 ** (-8.0 * (jnp.arange(n_heads, dtype=_F32) + 1.0) / n_heads)
    tp = _BF16_ROWS
    tables = _sample_bias_tables(slopes, n_t, tp, float(past_len))
    to_bf16 = lambda w: w.astype(_BF16)
    keep = lambda w: w
    cache_kt = jnp.transpose(cache_k, (0, 1, 3, 4, 2))
    cache_vt = jnp.transpose(cache_v, (0, 1, 3, 4, 2))

    def tokens(zs, col):
        part = zs[:, col * d:(col + 1) * d].reshape(n_t, n_seq, n_heads, hd)
        return jnp.transpose(part, (1, 0, 2, 3))

    def head_major(x4):
        return jnp.pad(jnp.transpose(x4, (0, 2, 1, 3)), ((0, 0), (0, 0), (0, tp - n_t), (0, 0)))

    xp = x_prompt
    xs = jnp.transpose(x_sample, (1, 0, 2)).reshape(n_t * n_seq, d)
    tile_p = 256
    tm_p = 1024

    kp_l, vp_l, ks_l, vs_l = [], [], [], []
    hp_l, hs_l, cp_l, cs_l = [], [], [], []
    pp_l, ps_l, dp_l, ds_l = [], [], [], []
    for layer in range(depth):
        li = layer // 2
        last = layer == depth - 1
        if layer % 2 == 0:
            lru_w = lambda cast: (lru_conv_w[li], lru_conv_b[li], cast(lru_wr[li]), lru_br[li], cast(lru_wi[li]),
                                  lru_bi[li], lru_lambda[li], cast(w_out_even[li]))
            z = _in_proj(xp.reshape(bp * seq, d), norm_even[li], to_bf16(w_in_even[li]), tm=tm_p, tn=512)
            z3 = z.reshape(bp, seq, z.shape[1])
            o = _moba_prompt(z3, slopes, q_col=2, k_col=3, v_col=4, w_b=w_b, hd=hd)
            xp, h_last = _even_out_prompt(z3, o, xp, *lru_w(to_bf16), tile=tile_p, w_a=d)
            kp_l.append(z3[:, :, 3 * d:4 * d].reshape(bp, seq, n_heads, hd))
            vp_l.append(z3[:, :, 4 * d:5 * d].reshape(bp, seq, n_heads, hd))
            hp_l.append(h_last.reshape(bp, d))
            cp_l.append(z3[:, seq - (lru_taps - 1):, 0:d])
            zs = _in_proj(xs, norm_even[li], w_in_even[li], tm=n_t * n_seq, tn=512)
            kn, vn = tokens(zs, 3), tokens(zs, 4)
            os_ = _moba_sample(head_major(tokens(zs, 2)), head_major(kn), head_major(vn), cache_kt, cache_vt,
                               page_table, tables, li=li)
            os_tm = jnp.transpose(os_[:, :, :n_t], (2, 0, 1, 3)).reshape(n_t * n_seq, w_b)
            xs, hs_last, cs_tm = _even_out_sample(zs, os_tm, xs, _time_major(state_lru_conv[:, li]),
                                                  state_lru_h[:, li], *lru_w(keep), n_seq=n_seq)
            ks_l.append(kn)
            vs_l.append(vn)
            hs_l.append(hs_last)
            cs_l.append(_batch_major(cs_tm, n_seq))
        else:
            odd_w = lambda cast: (cast(pool_w[li]), pool_b[li], pool_scale[li], dconv_w[li], dconv_b[li], dln_g[li],
                                  dln_b[li], cast(w_out_odd[li]), norm_final)
            z = _in_proj(xp.reshape(bp * seq, d), norm_odd[li], to_bf16(w_in_odd[li]), tm=tm_p, tn=512)
            z3 = z.reshape(bp, seq, z.shape[1])
            xp, dstate = _odd_out_prompt(z3, xp, *odd_w(to_bf16), tile=tile_p, final_norm=last)
            pp_l.append(z3[:, seq - pool_buf:, 0:d])
            dp_l.append(dstate[:, dstate.shape[1] - (dconv_taps - 1):, :])
            zs = _in_proj(xs, norm_odd[li], w_in_odd[li], tm=n_t * n_seq, tn=512)
            xs, ps_tm, ds_tm = _odd_out_sample(zs, xs, _time_major(state_pool[:, li]),
                                               _time_major(state_dconv[:, li]), *odd_w(keep), n_seq=n_seq,
                                               pos0=past_len, final_norm=last)
            ps_l.append(_batch_major(ps_tm, n_seq))
            ds_l.append(_batch_major(ds_tm, n_seq))

    if depth % 2 == 1:
        raise NotImplementedError("final norm is fused into the last odd layer")
    y_prompt = xp
    y_sample = jnp.transpose(xs.reshape(n_t, n_seq, d), (1, 0, 2))
    stack = lambda xs_: jnp.stack(xs_, axis=1)
    return (y_prompt, y_sample, stack(kp_l), stack(vp_l), stack(ks_l), stack(vs_l),
            stack(hp_l), stack(hs_l), stack(cp_l), stack(cs_l),
            stack(pp_l), stack(ps_l), stack(dp_l), stack(ds_l))
```

```python
import functools

import jax
import jax.numpy as jnp
from jax import lax
from jax.experimental import pallas as pl
from jax.experimental.pallas import tpu as pltpu

_MOBA_BLOCK = 256
_MOBA_TOPK = 3
_MOBA_HEAD_GROUP = 4
_MOBA_BLOCKS_PER_TRIP = 4
_SAMPLE_BLOCKS_PER_STEP = 4
_N_HEADS = 16
_LRU_BLOCKS = 8
_LRU_C = 8.0
_POOL_WINDOWS = (2, 4, 8, 16)
_EPS = 1e-6
_NEG = -1e30

_LANES = 128
_SUBLANES = 8
_BF16_ROWS = 16
_LOG2E = 1.4426950408889634
_BIAS_TERMS = 3
_VMEM_LIMIT_BYTES = 56 * 1024 * 1024

_BF16 = jnp.bfloat16
_F32 = jnp.float32
_NT = (((1,), (1,)), ((), ()))


def _cparams(*sem):
    return pltpu.CompilerParams(dimension_semantics=sem, vmem_limit_bytes=_VMEM_LIMIT_BYTES)


def _mxu(x, w):
    if w.dtype == _BF16:
        return jnp.dot(x.astype(_BF16), w, preferred_element_type=_F32)
    return jnp.dot(x, w, precision=lax.Precision.HIGHEST, preferred_element_type=_F32)


def _silu(x):
    return x * jax.nn.sigmoid(x)


def _rms(x, g):
    return x * lax.rsqrt(jnp.mean(x * x, axis=-1, keepdims=True) + _EPS) * g


def _in_proj_kernel(x_ref, g_ref, w_ref, z_ref, h_ref):
    @pl.when(pl.program_id(1) == 0)
    def _():
        h_ref[...] = _rms(x_ref[...], g_ref[...]).astype(h_ref.dtype)

    z_ref[...] = _mxu(h_ref[...], w_ref[...])


def _in_proj(x, g, w, *, tm, tn):
    m, d = x.shape
    n = w.shape[1]
    assert m % tm == 0 and n % tn == 0
    return pl.pallas_call(
        _in_proj_kernel,
        grid=(m // tm, n // tn),
        in_specs=[
            pl.BlockSpec((tm, d), lambda i, j: (i, 0)),
            pl.BlockSpec((1, d), lambda i, j: (0, 0)),
            pl.BlockSpec((d, tn), lambda i, j: (0, j)),
        ],
        out_specs=pl.BlockSpec((tm, tn), lambda i, j: (i, j)),
        out_shape=jax.ShapeDtypeStruct((m, n), _F32),
        scratch_shapes=[pltpu.VMEM((tm, d), w.dtype)],
        compiler_params=_cparams("parallel", "arbitrary"),
        name="in_proj",
    )(x, g.reshape(1, d), w)


def _topk_select(gate, blk_iota, nblk, axis):
    sel = jnp.zeros(gate.shape, jnp.bool_)
    g = gate
    for _ in range(_MOBA_TOPK):
        mx = jnp.max(g, axis=axis, keepdims=True)
        first = jnp.min(jnp.where(g == mx, blk_iota, nblk), axis=axis, keepdims=True)
        pick = blk_iota == first
        sel = jnp.logical_or(sel, pick)
        g = jnp.where(pick, -jnp.inf, g)
    return sel


def _moba_prompt_kernel(slopes_ref, q_ref, k_ref, v_ref, o_ref,
                        kb_ref, vt_ref, km_ref, qb_ref, pen_ref, acc_ref, *, hd):
    blk = _MOBA_BLOCK
    hp = pl.program_id(1)
    qi = pl.program_id(2)
    nblk = km_ref.shape[0]
    width = q_ref.shape[2]
    n_h = width // hd
    kw = kb_ref.shape[2]
    va = vt_ref.shape[2]

    @pl.when(qi == 0)
    def _():
        ones_rows = (lax.broadcasted_iota(jnp.int32, (va - hd, blk), 0) == 0).astype(_F32)
        lane = lax.broadcasted_iota(jnp.int32, (blk, kw), 1)
        key_r = lax.broadcasted_iota(jnp.int32, (blk, kw), 0).astype(_F32)
        feat = jnp.where(jnp.logical_and(lane >= hd, lane < hd + _BIAS_TERMS), key_r, 0.0)
        for j in range(nblk):
            kj = k_ref[0, j * blk:(j + 1) * blk, :]
            km_ref[j:j + 1, :] = jnp.sum(kj, axis=0, keepdims=True) * (1.0 / blk)
            vt = v_ref[0, j * blk:(j + 1) * blk, :].T
            for hh in range(n_h):
                lo = hh * hd // kw * kw
                chunk = kj[:, lo:lo + kw]
                if hh * hd != lo:
                    chunk = pltpu.roll(chunk, kw - (hh * hd - lo), 1)
                kb_ref[hh, j * blk:(j + 1) * blk, :] = jnp.where(lane < hd, chunk, feat).astype(_BF16)
                vt_ref[j, hh] = jnp.concatenate([vt[hh * hd:(hh + 1) * hd, :], ones_rows], axis=0).astype(_BF16)

    qt = q_ref[0].T
    term_row = lax.broadcasted_iota(jnp.int32, (kw - hd, blk), 0)
    d_iota = lax.broadcasted_iota(jnp.int32, (width, blk), 0)
    blk_iota = lax.broadcasted_iota(jnp.int32, (nblk, blk), 0)
    key_i = lax.broadcasted_iota(jnp.int32, (blk, blk), 0)
    qry_i = lax.broadcasted_iota(jnp.int32, (blk, blk), 1)
    past = blk_iota < qi
    q_start = pl.multiple_of(qi * blk, blk)
    slopes = [slopes_ref[hp * n_h + hh] * _LOG2E for hh in range(n_h)]

    m_init = []
    for hh in range(n_h):
        qm = jnp.where(jnp.logical_and(d_iota >= hh * hd, d_iota < (hh + 1) * hd), qt, 0.0)
        gate = jnp.dot(km_ref[...], qm, precision=lax.Precision.HIGHEST, preferred_element_type=_F32)
        sel = _topk_select(jnp.where(past, gate, _NEG), blk_iota, nblk, axis=0)
        pen_ref[hh] = jnp.where(jnp.logical_and(sel, past), 0.0, _NEG)
        rest = jnp.full((kw - hd, blk), slopes[hh], _F32)
        terms = jnp.zeros((kw - hd, blk), _F32)
        for i in range(_BIAS_TERMS):
            part = rest.astype(_BF16).astype(_F32)
            terms = jnp.where(term_row == i, part, terms)
            rest = rest - part
        qb = jnp.concatenate([qt[hh * hd:(hh + 1) * hd, :] * (hd ** -0.5 * _LOG2E), terms], axis=0).astype(_BF16)
        qb_ref[hh] = qb
        s = jnp.dot(kb_ref[hh, pl.ds(q_start, blk), :], qb, preferred_element_type=_F32)
        s = jnp.where(key_i <= qry_i, s, _NEG)
        m0 = jnp.max(s, axis=0, keepdims=True)
        m_init.append(m0)
        acc_ref[hh] = jnp.dot(vt_ref[qi, hh], jnp.exp2(s - m0).astype(_BF16), preferred_element_type=_F32)

    grp = _MOBA_BLOCKS_PER_TRIP

    def body(jj, ms):
        j0 = grp * jj
        rows = pl.ds(pl.multiple_of(j0 * blk, grp * blk), grp * blk)
        sg = [jnp.dot(kb_ref[hh, rows, :], qb_ref[hh], preferred_element_type=_F32) for hh in range(n_h)]
        new_m = []
        for hh in range(n_h):
            pen = pen_ref[hh]
            ts, offs = [], []
            for u in range(grp):
                pen_u = jnp.sum(jnp.where(blk_iota == j0 + u, pen, 0.0), axis=0, keepdims=True)
                offs.append(pen_u - slopes[hh] * ((qi - j0 - u) * blk).astype(_F32))
                ts.append(sg[hh][u * blk:(u + 1) * blk, :])
            m_new = ms[hh]
            for u in range(grp):
                m_new = jnp.maximum(m_new, jnp.max(ts[u], axis=0, keepdims=True) + offs[u])
            alpha = jnp.exp2(ms[hh] - m_new)
            pv = None
            for u in range(grp):
                pu = jnp.dot(vt_ref[j0 + u, hh], jnp.exp2(ts[u] - (m_new - offs[u])).astype(_BF16),
                             preferred_element_type=_F32)
                pv = pu if pv is None else pv + pu
            acc_ref[hh] = alpha * acc_ref[hh] + pv
            new_m.append(m_new)
        return tuple(new_m)

    lax.fori_loop(0, (qi + grp - 1) // grp, body, tuple(m_init))
    outs = []
    for hh in range(n_h):
        acc = acc_ref[hh]
        outs.append(acc[0:hd, :] / acc[hd:hd + 1, :])
    o_ref[0] = jnp.concatenate(outs, axis=0).T


def _moba_prompt(z3, slopes, *, q_col, k_col, v_col, w_b, hd):
    b, s, _ = z3.shape
    width = _MOBA_HEAD_GROUP * hd
    n_h = width // hd
    groups = w_b // width
    nblk = s // _MOBA_BLOCK
    assert nblk % _MOBA_BLOCKS_PER_TRIP == 0 and w_b % width == 0
    kern = functools.partial(_moba_prompt_kernel, hd=hd)
    resident = lambda col: pl.BlockSpec((1, s, width), lambda bi, hp, qi, sl: (bi, 0, col * groups + hp),
                                        pipeline_mode=pl.Buffered(1))
    grid_spec = pltpu.PrefetchScalarGridSpec(
        num_scalar_prefetch=1,
        grid=(b, groups, nblk),
        in_specs=[
            pl.BlockSpec((1, _MOBA_BLOCK, width), lambda bi, hp, qi, sl: (bi, qi, q_col * groups + hp)),
            resident(k_col),
            resident(v_col),
        ],
        out_specs=pl.BlockSpec((1, _MOBA_BLOCK, width), lambda bi, hp, qi, sl: (bi, qi, hp)),
        scratch_shapes=[
            pltpu.VMEM((n_h, s, _LANES), _BF16),
            pltpu.VMEM((nblk, n_h, hd + _BF16_ROWS, _MOBA_BLOCK), _BF16),
            pltpu.VMEM((nblk, width), _F32),
            pltpu.VMEM((n_h, _LANES, _MOBA_BLOCK), _BF16),
            pltpu.VMEM((n_h, nblk, _MOBA_BLOCK), _F32),
            pltpu.VMEM((n_h, hd + _BF16_ROWS, _MOBA_BLOCK), _F32),
        ],
    )
    return pl.pallas_call(
        kern,
        grid_spec=grid_spec,
        out_shape=jax.ShapeDtypeStruct((b, s, w_b), _F32),
        compiler_params=_cparams("parallel", "parallel", "arbitrary"),
        name="moba_prompt",
    )(slopes, z3, z3, z3)


def _moba_sample_kernel(pt_ref, q_ref, kn_ref, vn_ref, *refs):
    n_pg = 4 * _SAMPLE_BLOCKS_PER_STEP
    page_refs = refs[:n_pg]
    kbias_ref, bown_ref, slope_ref, tpos_ref, o_ref, qb_ref, ks_ref, m_ref, l_ref, acc_ref = refs[n_pg:]
    blk = _MOBA_BLOCK
    step = pl.program_id(1)
    n_steps = pl.num_programs(1)
    _, n_h, tp, hd = q_ref.shape
    nb = m_ref.shape[2]
    blk_iota = lax.broadcasted_iota(jnp.int32, (n_h, tp, nb), 2)
    ks_iota = lax.broadcasted_iota(jnp.int32, (n_h, hd, nb), 2)
    qk = (((2,), (1,)), ((0,), (0,)))
    pv = (((2,), (2,)), ((0,), (0,)))

    @pl.when(step == 0)
    def _():
        qb_ref[...] = (q_ref[0] * (hd ** -0.5)).astype(_BF16)
        m_ref[...] = jnp.zeros(m_ref.shape, _F32)
        l_ref[...] = jnp.zeros(l_ref.shape, _F32)
        ks_ref[...] = jnp.zeros(ks_ref.shape, _F32)

    for u in range(_SAMPLE_BLOCKS_PER_STEP):
        j = step * _SAMPLE_BLOCKS_PER_STEP + u
        k0_ref, k1_ref, v0_ref, v1_ref = page_refs[4 * u:4 * u + 4]
        k = jnp.concatenate([k0_ref[0, 0], k1_ref[0, 0]], axis=2)
        v = jnp.concatenate([v0_ref[0, 0], v1_ref[0, 0]], axis=2)
        ks_ref[...] = jnp.where(ks_iota == j, jnp.sum(k, axis=2, keepdims=True), ks_ref[...])
        s = lax.dot_general(qb_ref[...], k.astype(_BF16), qk, preferred_element_type=_F32) + kbias_ref[...]
        m_loc = jnp.max(s, axis=2, keepdims=True)
        p = jnp.exp(s - m_loc)
        l_loc = jnp.sum(p, axis=2, keepdims=True)
        acc_ref[j] = lax.dot_general(p.astype(_BF16), v.astype(_BF16), pv, preferred_element_type=_F32)
        m_blk = m_loc - slope_ref[...] * (tpos_ref[...] - (j * blk).astype(_F32))
        m_ref[...] = jnp.where(blk_iota == j, m_blk, m_ref[...])
        l_ref[...] = jnp.where(blk_iota == j, l_loc, l_ref[...])

    @pl.when(step == n_steps - 1)
    def _():
        gate = lax.dot_general(q_ref[0], ks_ref[...] * (1.0 / blk), qk, precision=lax.Precision.HIGHEST,
                               preferred_element_type=_F32)
        sel = _topk_select(gate, blk_iota, nb, axis=2)
        s_own = lax.dot_general(qb_ref[...], kn_ref[0].astype(_BF16), pv,
                                preferred_element_type=_F32) + bown_ref[...]
        m_all = m_ref[...]
        m_fin = jnp.maximum(jnp.max(s_own, axis=2, keepdims=True),
                            jnp.max(jnp.where(sel, m_all, -jnp.inf), axis=2, keepdims=True))
        w = jnp.where(sel, jnp.exp(m_all - m_fin), 0.0)
        p_own = jnp.exp(s_own - m_fin)
        denom = jnp.sum(w * l_ref[...], axis=2, keepdims=True) + jnp.sum(p_own, axis=2, keepdims=True)
        out = lax.dot_general(p_own.astype(_BF16), vn_ref[0].astype(_BF16), qk, preferred_element_type=_F32)

        def merge(jj, out):
            w_j = jnp.sum(jnp.where(blk_iota == jj, w, 0.0), axis=2, keepdims=True)
            return out + w_j * acc_ref[jj]

        o_ref[0] = lax.fori_loop(0, nb, merge, out) / denom


def _sample_bias_tables(slopes, n_t, tp, past_len):
    n_h = slopes.shape[0]
    key = jnp.arange(_MOBA_BLOCK, dtype=_F32)
    kbias = jnp.broadcast_to(slopes[:, None, None] * key[None, None, :], (n_h, tp, _MOBA_BLOCK))
    t = jnp.arange(tp)[:, None]
    r = jnp.arange(tp)[None, :]
    ok = jnp.logical_and(r <= t, r < n_t)
    bown = jnp.where(ok[None], -slopes[:, None, None] * (t - r).astype(_F32)[None], _NEG)
    slope3 = jnp.broadcast_to(slopes[:, None, None], (n_h, tp, 1))
    pos3 = jnp.broadcast_to(past_len + t.astype(_F32)[None], (n_h, tp, 1))
    return kbias.astype(_F32), bown.astype(_F32), slope3.astype(_F32), pos3.astype(_F32)


def _moba_sample(q, kn, vn, cache_kt, cache_vt, page_table, tables, *, li):
    n_seq, n_h, tp, hd = q.shape
    page_size = cache_kt.shape[4]
    n_pages = page_table.shape[1]
    assert _MOBA_BLOCK == 2 * page_size
    nb = n_pages // 2
    per_step = _SAMPLE_BLOCKS_PER_STEP
    assert nb % per_step == 0
    kern = _moba_sample_kernel
    tok = pl.BlockSpec((1, n_h, tp, hd), lambda n, j, pt: (n, 0, 0, 0))
    page = lambda i: pl.BlockSpec((1, 1, n_h, hd, page_size),
                                  lambda n, j, pt: (pt[n, 2 * per_step * j + i], li, 0, 0, 0))
    const = lambda arr: pl.BlockSpec(arr.shape, lambda n, j, pt: (0,) * arr.ndim)
    pages, operands = [], []
    for u in range(per_step):
        pages += [page(2 * u), page(2 * u + 1), page(2 * u), page(2 * u + 1)]
        operands += [cache_kt, cache_kt, cache_vt, cache_vt]
    grid_spec = pltpu.PrefetchScalarGridSpec(
        num_scalar_prefetch=1,
        grid=(n_seq, nb // per_step),
        in_specs=[tok, tok, tok] + pages + [const(t) for t in tables],
        out_specs=tok,
        scratch_shapes=[
            pltpu.VMEM((n_h, tp, hd), _BF16),
            pltpu.VMEM((n_h, hd, nb), _F32),
            pltpu.VMEM((n_h, tp, nb), _F32),
            pltpu.VMEM((n_h, tp, nb), _F32),
            pltpu.VMEM((nb, n_h, tp, hd), _F32),
        ],
    )
    return pl.pallas_call(
        kern,
        grid_spec=grid_spec,
        out_shape=jax.ShapeDtypeStruct((n_seq, n_h, tp, hd), _F32),
        compiler_params=_cparams("parallel", "arbitrary"),
        name="moba_sample",
    )(page_table, q, kn, vn, *operands, *tables)


def _lru_gates(xc, wr_ref, br, wi_ref, bi, lam):
    bw = wr_ref.shape[1]
    xb = xc.astype(wr_ref.dtype)
    r_parts, i_parts = [], []
    for g in range(wr_ref.shape[0]):
        xg = xb[:, g * bw:(g + 1) * bw]
        r_parts.append(_mxu(xg, wr_ref[g]))
        i_parts.append(_mxu(xg, wi_ref[g]))
    r = jax.nn.sigmoid(jnp.concatenate(r_parts, axis=1) + br)
    i = jax.nn.sigmoid(jnp.concatenate(i_parts, axis=1) + bi)
    softplus_neg_lam = jnp.maximum(-lam, 0.0) + jnp.log1p(jnp.exp(-jnp.abs(lam)))
    log_a = (-_LRU_C * softplus_neg_lam) * r
    a = jnp.exp(log_a)
    bx = jnp.sqrt(1.0 - a * a) * (i * xc)
    return a, bx


def _scan_rows(a, b):
    n = a.shape[0]
    row = lax.broadcasted_iota(jnp.int32, a.shape, 0)
    s = 1
    while s < n:
        keep = row >= s
        a_sh = jnp.where(keep, pltpu.roll(a, s, 0), 1.0)
        b_sh = jnp.where(keep, pltpu.roll(b, s, 0), 0.0)
        b = a * b_sh + b
        a = a * a_sh
        s *= 2
    return a, b


def _out_proj(mix_a, mix_b, wout_ref, x):
    half = wout_ref.shape[0] // 2
    return x + _mxu(mix_a, wout_ref[0:half, :]) + _mxu(mix_b, wout_ref[half:2 * half, :])


def _even_out_prompt_kernel(xa_ref, ga_ref, gb_ref, o_ref, x_ref, cw_ref, cb_ref, wr_ref, br_ref, wi_ref,
                            bi_ref, lam_ref, wout_ref, y_ref, hlast_ref, ext_ref, h_ref):
    tile = xa_ref.shape[1]
    pre = _SUBLANES
    taps = cw_ref.shape[0]

    @pl.when(pl.program_id(1) == 0)
    def _():
        ext_ref[0:pre, :] = jnp.zeros((pre, ext_ref.shape[1]), _F32)
        h_ref[...] = jnp.zeros(h_ref.shape, _F32)

    xa = xa_ref[0]
    ext_ref[pre:pre + tile, :] = xa
    xc = cb_ref[...] + cw_ref[taps - 1:taps, :] * xa
    for k in range(taps - 1):
        off = pre - (taps - 1) + k
        xc = xc + cw_ref[k:k + 1, :] * ext_ref[off:off + tile, :]
    ext_ref[0:pre, :] = xa[tile - pre:tile, :]

    a, bx = _lru_gates(xc, wr_ref, br_ref[...], wi_ref, bi_ref[...], lam_ref[...])
    a_cum, h = _scan_rows(a, bx)
    h = h + a_cum * h_ref[...]
    h_ref[...] = h[tile - 1:tile, :]
    hlast_ref[0] = h[tile - 1:tile, :]

    y_ref[0] = _out_proj(h * _silu(ga_ref[0]), o_ref[0] * _silu(gb_ref[0]), wout_ref, x_ref[0])


def _even_out_prompt(z3, o, x, cw, cb, wr, br, wi, bi, lam, wout, *, tile, w_a):
    b, s, d = x.shape
    nc = w_a // d
    assert nc == 1
    row = lambda col: pl.BlockSpec((1, tile, d), lambda bi_, t: (bi_, t, col))
    full = lambda arr: pl.BlockSpec(arr.shape, lambda bi_, t: (0,) * arr.ndim)
    vec = pl.BlockSpec((1, d), lambda bi_, t: (0, 0))
    return pl.pallas_call(
        _even_out_prompt_kernel,
        grid=(b, s // tile),
        in_specs=[row(0), row(1), row(5), row(0), row(0), full(cw), vec, full(wr), vec, full(wi), vec, vec,
                  full(wout)],
        out_specs=[row(0), pl.BlockSpec((1, 1, d), lambda bi_, t: (bi_, 0, 0))],
        out_shape=[jax.ShapeDtypeStruct((b, s, d), _F32), jax.ShapeDtypeStruct((b, 1, d), _F32)],
        scratch_shapes=[pltpu.VMEM((_SUBLANES + tile, d), _F32), pltpu.VMEM((1, d), _F32)],
        compiler_params=_cparams("parallel", "arbitrary"),
        name="even_out_prompt",
    )(z3, z3, z3, o, x, cw, cb.reshape(1, d), wr, br.reshape(1, d), wi, bi.reshape(1, d), lam.reshape(1, d), wout)


def _even_out_sample_kernel(z_ref, o_ref, x_ref, pre_ref, h0_ref, cw_ref, cb_ref, wr_ref, br_ref, wi_ref,
                            bi_ref, lam_ref, wout_ref, y_ref, hlast_ref, cstate_ref, *, n_seq):
    d = x_ref.shape[1]
    rows = x_ref.shape[0]
    taps = cw_ref.shape[0]
    xa = z_ref[:, 0:d]
    ext = jnp.concatenate([pre_ref[...], xa], axis=0)
    xc = cb_ref[...]
    for k in range(taps):
        xc = xc + cw_ref[k:k + 1, :] * ext[k * n_seq:k * n_seq + rows, :]
    cstate_ref[...] = ext[rows:rows + (taps - 1) * n_seq, :]

    a, bx = _lru_gates(xc, wr_ref, br_ref[...], wi_ref, bi_ref[...], lam_ref[...])
    h = h0_ref[...]
    hs = []
    for t in range(rows // n_seq):
        h = a[t * n_seq:(t + 1) * n_seq, :] * h + bx[t * n_seq:(t + 1) * n_seq, :]
        hs.append(h)
    hlast_ref[...] = h
    ha = jnp.concatenate(hs, axis=0)
    y_ref[...] = _out_proj(ha * _silu(z_ref[:, d:2 * d]), o_ref[...] * _silu(z_ref[:, 5 * d:6 * d]), wout_ref,
                           x_ref[...])


def _even_out_sample(z, o, x, pre, h0, cw, cb, wr, br, wi, bi, lam, wout, *, n_seq):
    rows, d = x.shape
    args = (z, o, x, pre, h0, cw, cb.reshape(1, d), wr, br.reshape(1, d), wi, bi.reshape(1, d), lam.reshape(1, d),
            wout)
    full = lambda arr: pl.BlockSpec(arr.shape, lambda i: (0,) * arr.ndim)
    once = lambda arr: pl.BlockSpec(arr.shape, lambda i: (0,) * arr.ndim, pipeline_mode=pl.Buffered(1))
    outs = [jax.ShapeDtypeStruct((rows, d), _F32), jax.ShapeDtypeStruct((n_seq, d), _F32),
            jax.ShapeDtypeStruct(pre.shape, _F32)]
    return pl.pallas_call(
        functools.partial(_even_out_sample_kernel, n_seq=n_seq),
        grid=(1,),
        in_specs=[once(a) for a in args],
        out_specs=[full(o_) for o_ in outs],
        out_shape=outs,
        compiler_params=_cparams("arbitrary"),
        name="even_out_sample",
    )(*args)


def _pool_project(pooled, pw_ref, pb, ps):
    gw = pw_ref.shape[1]
    pooled = pooled.astype(pw_ref.dtype)
    parts = [_mxu(pooled[:, g * gw:(g + 1) * gw], pw_ref[g]) for g in range(pw_ref.shape[0])]
    return (jnp.concatenate(parts, axis=1) + pb) * ps


def _layernorm(x, g, b):
    xc = x - jnp.mean(x, axis=-1, keepdims=True)
    var = jnp.mean(xc * xc, axis=-1, keepdims=True)
    return xc * lax.rsqrt(var + _EPS) * g + b


def _odd_out_prompt_kernel(xc_ref, gc_ref, da_ref, db_ref, gd_ref, x_ref, pw_ref, pb_ref, ps_ref, dw_ref,
                           dbias_ref, lg_ref, lb_ref, wout_ref, nf_ref, y_ref, dstate_ref, pext_ref, uext_ref,
                           *, final_norm):
    tile = xc_ref.shape[1]
    d = xc_ref.shape[2]
    ppre = pext_ref.shape[0] - tile
    upre = uext_ref.shape[0] - tile
    taps = dw_ref.shape[0]
    t_idx = pl.program_id(1)

    @pl.when(t_idx == 0)
    def _():
        pext_ref[0:ppre, :] = jnp.zeros((ppre, d), _F32)
        uext_ref[0:upre, :] = jnp.zeros((upre, d), _F32)

    xc = xc_ref[0]
    pext_ref[ppre:ppre + tile, :] = xc
    pos = t_idx * tile + lax.broadcasted_iota(jnp.int32, (tile, 1), 0)
    gw = d // len(_POOL_WINDOWS)
    parts = []
    for g, win in enumerate(_POOL_WINDOWS):
        ssum = xc[:, g * gw:(g + 1) * gw]
        for sft in range(1, win):
            ssum = ssum + pext_ref[ppre - sft:ppre - sft + tile, g * gw:(g + 1) * gw]
        cnt = jnp.minimum(win, pos + 1).astype(_F32)
        parts.append(ssum / cnt)
    pooled = jnp.concatenate(parts, axis=1) - xc
    pext_ref[0:ppre, :] = xc[tile - ppre:tile, :]
    yc = _pool_project(pooled, pw_ref, pb_ref[...], ps_ref[...])

    u = da_ref[0] * jax.nn.sigmoid(db_ref[0])
    uext_ref[upre:upre + tile, :] = u
    sub = _SUBLANES
    dc = dbias_ref[...]
    for b in range(min(sub, taps)):
        inner = None
        for s in range(b, taps, sub):
            lo = upre - sub - (s - b)
            term = dw_ref[taps - 1 - s:taps - s, :] * uext_ref[lo:lo + tile + sub, :]
            inner = term if inner is None else inner + term
        dc = dc + inner[sub - b:sub - b + tile, :]
    uext_ref[0:upre, :] = u[tile - upre:tile, :]
    dstate_ref[0] = u[tile - upre:tile, :]
    yd = _silu(_layernorm(dc, lg_ref[...], lb_ref[...]))

    y = _out_proj(yc * _silu(gc_ref[0]), yd * _silu(gd_ref[0]), wout_ref, x_ref[0])
    if final_norm:
        y = _rms(y, nf_ref[...])
    y_ref[0] = y


def _odd_out_prompt(z3, x, pw, pb, ps, dw, dbias, lg, lb, wout, nf, *, tile, final_norm):
    b, s, d = x.shape
    ppre = 16
    upre = 32
    assert max(_POOL_WINDOWS) - 1 <= ppre
    assert _SUBLANES + (dw.shape[0] - 1) // _SUBLANES * _SUBLANES <= upre
    row = lambda col: pl.BlockSpec((1, tile, d), lambda bi_, t: (bi_, t, col))
    full = lambda arr: pl.BlockSpec(arr.shape, lambda bi_, t: (0,) * arr.ndim)
    vec = pl.BlockSpec((1, d), lambda bi_, t: (0, 0))
    return pl.pallas_call(
        functools.partial(_odd_out_prompt_kernel, final_norm=final_norm),
        grid=(b, s // tile),
        in_specs=[row(0), row(1), row(2), row(3), row(4), row(0), full(pw), vec, vec, full(dw), vec, vec, vec,
                  full(wout), vec],
        out_specs=[row(0), pl.BlockSpec((1, upre, d), lambda bi_, t: (bi_, 0, 0))],
        out_shape=[jax.ShapeDtypeStruct((b, s, d), _F32), jax.ShapeDtypeStruct((b, upre, d), _F32)],
        scratch_shapes=[pltpu.VMEM((ppre + tile, d), _F32), pltpu.VMEM((upre + tile, d), _F32)],
        compiler_params=_cparams("parallel", "arbitrary"),
        name="odd_out_prompt",
    )(z3, z3, z3, z3, z3, x, pw, pb.reshape(1, d), ps.reshape(1, d), dw, dbias.reshape(1, d), lg.reshape(1, d),
      lb.reshape(1, d), wout, nf.reshape(1, d))


def _odd_out_sample_kernel(z_ref, x_ref, ppre_ref, upre_ref, pw_ref, pb_ref, ps_ref, dw_ref, dbias_ref, lg_ref,
                           lb_ref, wout_ref, nf_ref, y_ref, pstate_ref, dstate_ref, *, n_seq, pos0, final_norm):
    rows, d = x_ref.shape
    taps = dw_ref.shape[0]
    pbuf = ppre_ref.shape[0] // n_seq
    xc = z_ref[:, 0:d]
    pext = jnp.concatenate([ppre_ref[...], xc], axis=0)
    pstate_ref[...] = pext[rows:rows + pbuf * n_seq, :]
    t_of_row = lax.broadcasted_iota(jnp.int32, (rows, 1), 0) // n_seq
    gw = d // len(_POOL_WINDOWS)
    parts = []
    for g, win in enumerate(_POOL_WINDOWS):
        ssum = xc[:, g * gw:(g + 1) * gw]
        for sft in range(1, win):
            lo = (pbuf - sft) * n_seq
            ssum = ssum + pext[lo:lo + rows, g * gw:(g + 1) * gw]
        cnt = jnp.minimum(win, pos0 + t_of_row + 1).astype(_F32)
        parts.append(ssum / cnt)
    pooled = jnp.concatenate(parts, axis=1) - xc
    yc = _pool_project(pooled, pw_ref, pb_ref[...], ps_ref[...])

    u = z_ref[:, 2 * d:3 * d] * jax.nn.sigmoid(z_ref[:, 3 * d:4 * d])
    uext = jnp.concatenate([upre_ref[...], u], axis=0)
    dstate_ref[...] = uext[rows:rows + (taps - 1) * n_seq, :]
    dc = dbias_ref[...]
    for k in range(taps):
        dc = dc + dw_ref[k:k + 1, :] * uext[k * n_seq:k * n_seq + rows, :]
    yd = _silu(_layernorm(dc, lg_ref[...], lb_ref[...]))

    y = _out_proj(yc * _silu(z_ref[:, d:2 * d]), yd * _silu(z_ref[:, 4 * d:5 * d]), wout_ref, x_ref[...])
    if final_norm:
        y = _rms(y, nf_ref[...])
    y_ref[...] = y


def _odd_out_sample(z, x, ppre, upre, pw, pb, ps, dw, dbias, lg, lb, wout, nf, *, n_seq, pos0, final_norm):
    rows, d = x.shape
    args = (z, x, ppre, upre, pw, pb.reshape(1, d), ps.reshape(1, d), dw, dbias.reshape(1, d), lg.reshape(1, d),
            lb.reshape(1, d), wout, nf.reshape(1, d))
    full = lambda arr: pl.BlockSpec(arr.shape, lambda i: (0,) * arr.ndim)
    once = lambda arr: pl.BlockSpec(arr.shape, lambda i: (0,) * arr.ndim, pipeline_mode=pl.Buffered(1))
    outs = [jax.ShapeDtypeStruct((rows, d), _F32), jax.ShapeDtypeStruct(ppre.shape, _F32),
            jax.ShapeDtypeStruct(upre.shape, _F32)]
    return pl.pallas_call(
        functools.partial(_odd_out_sample_kernel, n_seq=n_seq, pos0=pos0, final_norm=final_norm),
        grid=(1,),
        in_specs=[once(a) for a in args],
        out_specs=[full(o_) for o_ in outs],
        out_shape=outs,
        compiler_params=_cparams("arbitrary"),
        name="odd_out_sample",
    )(*args)


def _time_major(state):
    n, k, c = state.shape
    return jnp.transpose(state, (1, 0, 2)).reshape(k * n, c)


def _batch_major(state_tm, n):
    kn, c = state_tm.shape
    return jnp.transpose(state_tm.reshape(kn // n, n, c), (1, 0, 2))


def kernel(x_prompt, x_sample, cache_k, cache_v, page_table, state_lru_h, state_lru_conv, state_pool, state_dconv, norm_even, w_in_even, lru_conv_w, lru_conv_b, lru_wr, lru_br, lru_wi, lru_bi, lru_lambda, w_out_even, norm_odd, w_in_odd, pool_w, pool_b, pool_scale, dconv_w, dconv_b, dln_g, dln_b, w_out_odd, norm_final):
    bp, seq, d = x_prompt.shape
    n_seq, n_t, _ = x_sample.shape
    n_even, n_odd = w_in_even.shape[0], w_in_odd.shape[0]
    depth = n_even + n_odd
    n_phys, _, page_size, n_heads, hd = cache_k.shape
    w_b = n_heads * hd
    n_pages = page_table.shape[1]
    past_len = n_pages * page_size
    lru_taps = lru_conv_w.shape[1]
    pool_buf = state_pool.shape[2]
    dconv_taps = dconv_w.shape[1]
    assert w_b == d and seq % _MOBA_BLOCK == 0 and past_len % _MOBA_BLOCK == 0 and n_t <= _SUBLANES
    assert seq // _MOBA_BLOCK >= _MOBA_TOPK and past_len // _MOBA_BLOCK >= _MOBA_TOPK

    slopes = 2.0 ** (-8.0 * (jnp.arange(n_heads, dtype=_F32) + 1.0) / n_heads)
    tp = _BF16_ROWS
    tables = _sample_bias_tables(slopes, n_t, tp, float(past_len))
    to_bf16 = lambda w: w.astype(_BF16)
    keep = lambda w: w
    cache_kt = jnp.transpose(cache_k, (0, 1, 3, 4, 2))
    cache_vt = jnp.transpose(cache_v, (0, 1, 3, 4, 2))

    def tokens(zs, col):
        part = zs[:, col * d:(col + 1) * d].reshape(n_t, n_seq, n_heads, hd)
        return jnp.transpose(part, (1, 0, 2, 3))

    def head_major(x4):
        return jnp.pad(jnp.transpose(x4, (0, 2, 1, 3)), ((0, 0), (0, 0), (0, tp - n_t), (0, 0)))

    xp = x_prompt
    xs = jnp.transpose(x_sample, (1, 0, 2)).reshape(n_t * n_seq, d)
    tile_p = 256
    tm_p = 1024
    tn_p = 1024

    kp_l, vp_l, ks_l, vs_l = [], [], [], []
    hp_l, hs_l, cp_l, cs_l = [], [], [], []
    pp_l, ps_l, dp_l, ds_l = [], [], [], []
    for layer in range(depth):
        li = layer // 2
        last = layer == depth - 1
        if layer % 2 == 0:
            lru_w = lambda cast: (lru_conv_w[li], lru_conv_b[li], cast(lru_wr[li]), lru_br[li], cast(lru_wi[li]),
                                  lru_bi[li], lru_lambda[li], cast(w_out_even[li]))
            z = _in_proj(xp.reshape(bp * seq, d), norm_even[li], to_bf16(w_in_even[li]), tm=tm_p, tn=tn_p)
            z3 = z.reshape(bp, seq, z.shape[1])
            o = _moba_prompt(z3, slopes, q_col=2, k_col=3, v_col=4, w_b=w_b, hd=hd)
            xp, h_last = _even_out_prompt(z3, o, xp, *lru_w(to_bf16), tile=tile_p, w_a=d)
            kp_l.append(z3[:, :, 3 * d:4 * d].reshape(bp, seq, n_heads, hd))
            vp_l.append(z3[:, :, 4 * d:5 * d].reshape(bp, seq, n_heads, hd))
            hp_l.append(h_last.reshape(bp, d))
            cp_l.append(z3[:, seq - (lru_taps - 1):, 0:d])
            zs = _in_proj(xs, norm_even[li], w_in_even[li], tm=n_t * n_seq, tn=512)
            kn, vn = tokens(zs, 3), tokens(zs, 4)
            os_ = _moba_sample(head_major(tokens(zs, 2)), head_major(kn), head_major(vn), cache_kt, cache_vt,
                               page_table, tables, li=li)
            os_tm = jnp.transpose(os_[:, :, :n_t], (2, 0, 1, 3)).reshape(n_t * n_seq, w_b)
            xs, hs_last, cs_tm = _even_out_sample(zs, os_tm, xs, _time_major(state_lru_conv[:, li]),
                                                  state_lru_h[:, li], *lru_w(keep), n_seq=n_seq)
            ks_l.append(kn)
            vs_l.append(vn)
            hs_l.append(hs_last)
            cs_l.append(_batch_major(cs_tm, n_seq))
        else:
            odd_w = lambda cast: (cast(pool_w[li]), pool_b[li], pool_scale[li], dconv_w[li], dconv_b[li], dln_g[li],
                                  dln_b[li], cast(w_out_odd[li]), norm_final)
            z = _in_proj(xp.reshape(bp * seq, d), norm_odd[li], to_bf16(w_in_odd[li]), tm=tm_p, tn=tn_p)
            z3 = z.reshape(bp, seq, z.shape[1])
            xp, dstate = _odd_out_prompt(z3, xp, *odd_w(to_bf16), tile=tile_p, final_norm=last)
            pp_l.append(z3[:, seq - pool_buf:, 0:d])
            dp_l.append(dstate[:, dstate.shape[1] - (dconv_taps - 1):, :])
            zs = _in_proj(xs, norm_odd[li], w_in_odd[li], tm=n_t * n_seq, tn=512)
            xs, ps_tm, ds_tm = _odd_out_sample(zs, xs, _time_major(state_pool[:, li]),
                                               _time_major(state_dconv[:, li]), *odd_w(keep), n_seq=n_seq,
                                               pos0=past_len, final_norm=last)
            ps_l.append(_batch_major(ps_tm, n_seq))
            ds_l.append(_batch_major(ds_tm, n_seq))

    if depth % 2 == 1:
        raise NotImplementedError("final norm is fused into the last odd layer")
    y_prompt = xp
    y_sample = jnp.transpose(xs.reshape(n_t, n_seq, d), (1, 0, 2))
    stack = lambda xs_: jnp.stack(xs_, axis=1)
    return (y_prompt, y_sample, stack(kp_l), stack(vp_l), stack(ks_l), stack(vs_l),
            stack(hp_l), stack(hs_l), stack(cp_l), stack(cs_l),
            stack(pp_l), stack(ps_l), stack(dp_l), stack(ds_l))
```
